```python
import jax, jax.numpy as jnp
from jax import lax
import numpy as np

D_MODEL = 1024
BATCH = 2
SEQ = 8192
DEPTH = 2
DEC_BATCH = 32
DEC_SEQ = 16
PAST_LEN = 4096

CHUNK = 64
EXPAND = 2
D_INNER = EXPAND * D_MODEL
SGU_CHUNK = 128
SGU_GROUPS = 8
SGU_GROUP_DIM = D_INNER // SGU_GROUPS
MLSTM_HEADS = 4
MLSTM_HEAD_DIM = D_INNER // MLSTM_HEADS
QKV_BLOCK = 4
QKV_NBLK = D_INNER // QKV_BLOCK
CONV_W = 4
N_MIXERS = 2
N_A = (DEPTH + 1) // 2
N_B = DEPTH // 2
RMS_EPS = 1e-6
LN_EPS = 1e-5

kernel_name = "chunk_sgu_mlstm_hybrid_step"


def _rmsnorm(x, g):
    xf = x.astype(jnp.float32)
    y = xf * lax.rsqrt(jnp.mean(xf * xf, axis=-1, keepdims=True) + RMS_EPS)
    return (y * g.astype(jnp.float32)).astype(x.dtype)


def _layernorm(x, g, b=None):
    xf = x.astype(jnp.float32)
    mu = jnp.mean(xf, axis=-1, keepdims=True)
    var = jnp.mean(jnp.square(xf - mu), axis=-1, keepdims=True)
    y = (xf - mu) * lax.rsqrt(var + LN_EPS) * g.astype(jnp.float32)
    if b is not None:
        y = y + b.astype(jnp.float32)
    return y.astype(x.dtype)


def _sgu_branch(xn, w_in, ln_g, ln_b, w_s, b_s, w_out, prompt):
    E = D_INNER
    proj = xn @ w_in
    uv = jax.nn.gelu(proj[..., :2 * E])
    u, v = uv[..., :E], uv[..., E:]
    z = proj[..., 2 * E:]
    v = _layernorm(v, ln_g, ln_b)
    bsz, t_len, _ = v.shape
    causal = jnp.tril(jnp.ones((SGU_CHUNK, SGU_CHUNK), dtype=bool))
    w_m = jnp.where(causal, w_s, jnp.zeros_like(w_s))
    if prompt:
        vc = v.reshape(bsz, t_len // SGU_CHUNK, SGU_CHUNK, SGU_GROUPS, SGU_GROUP_DIM)
        s = jnp.einsum('gts,bnsgc->bntgc', w_m, vc) + b_s.T[:, :, None]
    else:
        vc = v.reshape(bsz, t_len, SGU_GROUPS, SGU_GROUP_DIM)
        s = jnp.einsum('gts,bsgc->btgc', w_m[:, :t_len, :t_len], vc) + b_s[:, :t_len].T[:, :, None]
    s = s.reshape(bsz, t_len, E)
    out = u * s * jax.nn.silu(z)
    return out @ w_out, v


def _mlstm_chunk(carry, inp):
    C, n, m = carry
    q, k, v, ig, lf = inp
    L = q.shape[-2]
    b = jnp.cumsum(lf, axis=-1)
    causal = jnp.tril(jnp.ones((L, L), dtype=bool))
    log_d = jnp.where(causal, b[..., :, None] - b[..., None, :] + ig[..., None, :], -jnp.inf)
    m_inter = b + m[..., None]
    m_t = jnp.maximum(m_inter, jnp.max(log_d, axis=-1))
    w_intra = jnp.exp(log_d - m_t[..., None])
    w_inter = jnp.exp(m_inter - m_t)
    s = jnp.einsum('bhtd,bhsd->bhts', q, k) * w_intra
    num = jnp.einsum('bhts,bhse->bhte', s, v) + w_inter[..., None] * jnp.einsum('bhtd,bhde->bhte', q, C)
    den = jnp.sum(s, axis=-1) + w_inter * jnp.einsum('bhtd,bhd->bht', q, n)
    h = num / jnp.maximum(jnp.abs(den), jnp.exp(-m_t))[..., None]
    b_last = b[..., -1]
    log_w = b_last[..., None] - b + ig
    m_new = jnp.maximum(b_last + m, jnp.max(log_w, axis=-1))
    w = jnp.exp(log_w - m_new[..., None])
    decay = jnp.exp(b_last + m - m_new)
    C_new = decay[..., None, None] * C + jnp.einsum('bhs,bhsd,bhse->bhde', w, k, v)
    n_new = decay[..., None] * n + jnp.einsum('bhs,bhsd->bhd', w, k)
    return (C_new, n_new, m_new), h


def _mlstm_branch(xn, w_in, conv_w, conv_b, wq, wk, wv, w_gates, b_gates, hn_g, skip, w_out,
                  C0, n0, m0, conv0):
    E, H, DH = D_INNER, MLSTM_HEADS, MLSTM_HEAD_DIM
    f32 = jnp.float32
    proj = xn @ w_in
    xm, z = proj[..., :E], proj[..., E:]
    bsz, t_len, _ = xm.shape
    xpad = jnp.concatenate([conv0.astype(xm.dtype), xm], axis=1)
    xc = conv_b + sum(xpad[:, j:j + t_len] * conv_w[j] for j in range(CONV_W))
    xc = jax.nn.silu(xc)
    new_conv = xpad[:, -(CONV_W - 1):]

    def blockdiag(a, w):
        a = a.reshape(bsz, t_len, QKV_NBLK, QKV_BLOCK)
        return jnp.einsum('btni,nio->btno', a, w).reshape(bsz, t_len, E)

    q, k, v = blockdiag(xc, wq), blockdiag(xc, wk), blockdiag(xm, wv)
    gates = (jnp.concatenate([q, k, v], axis=-1) @ w_gates + b_gates).astype(f32)
    ig = gates[..., :H]
    lf = jax.nn.log_sigmoid(gates[..., H:])

    def heads(a):
        return a.reshape(bsz, t_len, H, DH).transpose(0, 2, 1, 3).astype(f32)

    qh, kh, vh = heads(q), heads(k) * (DH ** -0.5), heads(v)
    igh, lfh = ig.transpose(0, 2, 1), lf.transpose(0, 2, 1)
    L = min(t_len, CHUNK)
    nc = t_len // L

    def chunks(a):
        return jnp.moveaxis(a.reshape(a.shape[:2] + (nc, L) + a.shape[3:]), 2, 0)

    (C, n, m), h = lax.scan(_mlstm_chunk, (C0.astype(f32), n0.astype(f32), m0.astype(f32)),
                            (chunks(qh), chunks(kh), chunks(vh), chunks(igh), chunks(lfh)))
    h = jnp.moveaxis(h, 0, 2).reshape(bsz, H, t_len, DH).transpose(0, 2, 1, 3)
    h = _layernorm(h, hn_g.reshape(H, DH)).reshape(bsz, t_len, E).astype(xm.dtype)
    out = (h + skip * xc) * jax.nn.silu(z)
    return out @ w_out, (C, n, m, new_conv)


def setup_inputs(seed: int = 0) -> dict:
    key = jax.random.key(seed)
    ks = jax.random.split(key, 32)
    E, H, DH, D = D_INNER, MLSTM_HEADS, MLSTM_HEAD_DIM, D_MODEL

    def nrm(k, shape, scale):
        return scale * jax.random.normal(k, shape, jnp.float32)

    f_bias = jnp.broadcast_to(jnp.linspace(3.0, 6.0, H, dtype=jnp.float32), (N_B, H))
    b_gates = jnp.concatenate([nrm(ks[20], (N_B, H), 0.1), f_bias + nrm(ks[21], (N_B, H), 0.1)], axis=-1)
    return {
        'x_prompt': nrm(ks[0], (BATCH, SEQ, D), 1.0),
        'x_sample': nrm(ks[1], (DEC_BATCH, DEC_SEQ, D), 1.0),
        'state_mlstm_C': nrm(ks[2], (N_B, DEC_BATCH, H, DH, DH), 0.05),
        'state_mlstm_n': nrm(ks[3], (N_B, DEC_BATCH, H, DH), 0.1),
        'state_mlstm_m': nrm(ks[4], (N_B, DEC_BATCH, H), 1.0),
        'state_mlstm_conv': nrm(ks[5], (N_B, DEC_BATCH, CONV_W - 1, E), 1.0),
        'norm_g': 1.0 + nrm(ks[6], (DEPTH, D), 0.05),
        'final_norm_g': 1.0 + nrm(ks[7], (D,), 0.05),
        'a_w_in': nrm(ks[8], (N_A, D, 3 * E), D ** -0.5),
        'a_ln_g': 1.0 + nrm(ks[9], (N_A, E), 0.05),
        'a_ln_b': nrm(ks[10], (N_A, E), 0.02),
        'a_w_s': nrm(ks[11], (N_A, SGU_GROUPS, SGU_CHUNK, SGU_CHUNK), 0.5 * SGU_CHUNK ** -0.5),
        'a_b_s': 1.0 + nrm(ks[12], (N_A, SGU_GROUPS, SGU_CHUNK), 0.1),
        'a_w_out': nrm(ks[13], (N_A, E, D), E ** -0.5),
        'b_w_in': nrm(ks[14], (N_B, D, 2 * E), D ** -0.5),
        'b_conv_w': nrm(ks[15], (N_B, CONV_W, E), CONV_W ** -0.5),
        'b_conv_b': nrm(ks[16], (N_B, E), 0.02),
        'b_wq': nrm(ks[17], (N_B, QKV_NBLK, QKV_BLOCK, QKV_BLOCK), QKV_BLOCK ** -0.5),
        'b_wk': nrm(ks[18], (N_B, QKV_NBLK, QKV_BLOCK, QKV_BLOCK), QKV_BLOCK ** -0.5),
        'b_wv': nrm(ks[19], (N_B, QKV_NBLK, QKV_BLOCK, QKV_BLOCK), QKV_BLOCK ** -0.5),
        'b_w_gates': nrm(ks[22], (N_B, 3 * E, 2 * H), (3 * E) ** -0.5),
        'b_b_gates': b_gates,
        'b_hnorm_g': 1.0 + nrm(ks[23], (N_B, E), 0.05),
        'b_skip': 1.0 + nrm(ks[24], (N_B, E), 0.05),
        'b_w_out': nrm(ks[25], (N_B, E, D), E ** -0.5),
    }


def reference(x_prompt, x_sample, state_mlstm_C, state_mlstm_n, state_mlstm_m, state_mlstm_conv,
              norm_g, final_norm_g, a_w_in, a_ln_g, a_ln_b, a_w_s, a_b_s, a_w_out,
              b_w_in, b_conv_w, b_conv_b, b_wq, b_wk, b_wv, b_w_gates, b_b_gates,
              b_hnorm_g, b_skip, b_w_out):
    E, H, DH = D_INNER, MLSTM_HEADS, MLSTM_HEAD_DIM
    xp, xs = x_prompt, x_sample
    sgu_v = []
    p_C, p_n, p_m, p_conv = [], [], [], []
    s_C, s_n, s_m, s_conv = [], [], [], []
    for i in range(DEPTH):
        j = i // N_MIXERS
        if i % N_MIXERS == 0:
            wa = (a_w_in[j], a_ln_g[j], a_ln_b[j], a_w_s[j], a_b_s[j], a_w_out[j])
            yp, _ = _sgu_branch(_rmsnorm(xp, norm_g[i]), *wa, prompt=True)
            ys, vs = _sgu_branch(_rmsnorm(xs, norm_g[i]), *wa, prompt=False)
            xp = xp + yp
            xs = xs + ys
            sgu_v.append(vs)
        else:
            wb = (b_w_in[j], b_conv_w[j], b_conv_b[j], b_wq[j], b_wk[j], b_wv[j],
                  b_w_gates[j], b_b_gates[j], b_hnorm_g[j], b_skip[j], b_w_out[j])
            zC = jnp.zeros((BATCH, H, DH, DH), jnp.float32)
            zn = jnp.zeros((BATCH, H, DH), jnp.float32)
            zm = jnp.zeros((BATCH, H), jnp.float32)
            zconv = jnp.zeros((BATCH, CONV_W - 1, E), xp.dtype)
            yp, (c1, n1, m1, cv1) = _mlstm_branch(_rmsnorm(xp, norm_g[i]), *wb, zC, zn, zm, zconv)
            ys, (c2, n2, m2, cv2) = _mlstm_branch(_rmsnorm(xs, norm_g[i]), *wb, state_mlstm_C[j],
                                                  state_mlstm_n[j], state_mlstm_m[j], state_mlstm_conv[j])
            xp = xp + yp
            xs = xs + ys
            p_C.append(c1); p_n.append(n1); p_m.append(m1); p_conv.append(cv1)
            s_C.append(c2); s_n.append(n2); s_m.append(m2); s_conv.append(cv2)
    y_prompt = _rmsnorm(xp, final_norm_g)
    y_sample = _rmsnorm(xs, final_norm_g)
    sdt = state_mlstm_C.dtype
    sgu_v_sample = jnp.stack(sgu_v).astype(x_sample.dtype)
    C_prompt = jnp.stack(p_C).astype(sdt)
    n_prompt = jnp.stack(p_n).astype(state_mlstm_n.dtype)
    m_prompt = jnp.stack(p_m).astype(state_mlstm_m.dtype)
    conv_prompt = jnp.stack(p_conv).astype(state_mlstm_conv.dtype)
    C_sample = jnp.stack(s_C).astype(sdt)
    n_sample = jnp.stack(s_n).astype(state_mlstm_n.dtype)
    m_sample = jnp.stack(s_m).astype(state_mlstm_m.dtype)
    conv_sample = jnp.stack(s_conv).astype(state_mlstm_conv.dtype)
    return (y_prompt, y_sample, sgu_v_sample, C_prompt, n_prompt, m_prompt, conv_prompt,
            C_sample, n_sample, m_sample, conv_sample)
```

```python
import functools

import jax
import jax.numpy as jnp
from jax import lax
from jax.experimental import pallas as pl
from jax.experimental.pallas import tpu as pltpu

D_MODEL = 1024
D_INNER = 2048
SGU_CHUNK = 128
SGU_GROUPS = 8
SGU_GROUP_DIM = D_INNER // SGU_GROUPS
HEADS = 4
HEAD_DIM = D_INNER // HEADS
QKV_BLOCK = 4
CONV_W = 4
RMS_EPS = 1e-6
LN_EPS = 1e-5

LANES = 128
SUBLANES = 8
MXU_DIM = 256
VMEM_LIMIT_BYTES = 56 * 1024 * 1024

BF16 = jnp.bfloat16
F32 = jnp.float32


def _dot(a, b):
    return jnp.dot(a, b, preferred_element_type=F32)


def _rmsnorm(x, g):
    return x * lax.rsqrt(jnp.mean(x * x, axis=-1, keepdims=True) + RMS_EPS) * g


def _layernorm(x, g):
    mu = jnp.mean(x, axis=-1, keepdims=True)
    xc = x - mu
    var = jnp.mean(xc * xc, axis=-1, keepdims=True)
    return xc * lax.rsqrt(var + LN_EPS) * g


def _log_sigmoid(x):
    return jnp.minimum(x, 0.0) - jnp.log1p(jnp.exp(-jnp.abs(x)))


def _const_spec(shape):
    zeros = (0,) * len(shape)
    return pl.BlockSpec(shape, lambda *_: zeros, pipeline_mode=pl.Buffered(1))


def _sgu_kernel(x_ref, ng_ref, win_ref, lng_ref, lnb_ref, ws_ref, bs_ref, wout_ref,
                y_ref, *v_refs, tm):
    e = D_INNER
    x = x_ref[...]
    xn = _rmsnorm(x, ng_ref[...]).astype(BF16)
    v = jax.nn.gelu(_dot(xn, win_ref[:, e:2 * e]))
    vn = _layernorm(v, lng_ref[...]) + lnb_ref[...]
    if v_refs:
        v_refs[0][...] = vn
    vb = vn.astype(BF16)
    ti = lax.broadcasted_iota(jnp.int32, (SGU_CHUNK, SGU_CHUNK), 0)
    si = lax.broadcasted_iota(jnp.int32, (SGU_CHUNK, SGU_CHUNK), 1)
    causal = si <= ti
    acc = jnp.zeros((tm, D_MODEL), F32)
    for g in range(SGU_GROUPS):
        lo, hi = g * SGU_GROUP_DIM, (g + 1) * SGU_GROUP_DIM
        u = jax.nn.gelu(_dot(xn, win_ref[:, lo:hi]))
        z = _dot(xn, win_ref[:, 2 * e + lo:2 * e + hi])
        wm = jnp.where(causal, ws_ref[g], 0.0).astype(BF16)
        bias = bs_ref[:, g:g + 1]
        s = jnp.concatenate(
            [_dot(wm, vb[c * SGU_CHUNK:(c + 1) * SGU_CHUNK, lo:hi]) + bias
             for c in range(tm // SGU_CHUNK)], axis=0)
        out = u * s * jax.nn.silu(z)
        acc = acc + _dot(out.astype(BF16), wout_ref[lo:hi, :])
    y_ref[...] = x + acc


def _sgu_layer(x, norm_g, w_in, ln_g, ln_b, w_s, b_s, w_out, *, tm, want_v):
    rows = x.shape[0]
    e = D_INNER
    row_spec = pl.BlockSpec((tm, D_MODEL), lambda i: (i, 0))
    out_shape = [jax.ShapeDtypeStruct((rows, D_MODEL), F32)]
    out_specs = [row_spec]
    if want_v:
        out_shape.append(jax.ShapeDtypeStruct((rows, e), F32))
        out_specs.append(pl.BlockSpec((tm, e), lambda i: (i, 0)))
    res = pl.pallas_call(
        functools.partial(_sgu_kernel, tm=tm),
        grid=(rows // tm,),
        in_specs=[row_spec,
                  _const_spec((1, D_MODEL)),
                  _const_spec((D_MODEL, 3 * e)),
                  _const_spec((1, e)),
                  _const_spec((1, e)),
                  _const_spec((SGU_GROUPS, SGU_CHUNK, SGU_CHUNK)),
                  _const_spec((SGU_CHUNK, SGU_GROUPS)),
                  _const_spec((e, D_MODEL))],
        out_specs=out_specs,
        out_shape=out_shape,
        compiler_params=pltpu.CompilerParams(
            dimension_semantics=("arbitrary",), vmem_limit_bytes=VMEM_LIMIT_BYTES),
        name="sgu_layer_v" if want_v else "sgu_layer",
    )(x, norm_g, w_in, ln_g, ln_b, w_s, b_s, w_out)
    return res


def _mlstm_kernel(*refs, ns, seg, has_state):
    (x_ref, ng_ref, fg_ref, win_ref, cw_ref, cb_ref, bdq_ref, bdk_ref, bdv_ref, wg_ref,
     bg_ref, hng_ref, skip_ref, wout_ref) = refs[:14]
    rest = refs[14:]
    if has_state:
        c0_ref, n0_ref, m0_ref, conv0_ref = rest[:4]
        rest = rest[4:]
    y_ref, c_ref, n_ref, m_ref, conv_ref, xpad_ref, h_ref = rest
    e, dh = D_INNER, HEAD_DIM
    pad = SUBLANES

    if has_state:
        cin_ref, nin_ref, min_ref = c0_ref, n0_ref, m0_ref
        m_ref[...] = jnp.zeros_like(m_ref)
        xpad_ref[:, pad - (CONV_W - 1):pad, :] = conv0_ref[...]
    else:
        cin_ref, nin_ref, min_ref = c_ref, n_ref, m_ref

        @pl.when(pl.program_id(1) == 0)
        def _():
            c_ref[...] = jnp.zeros_like(c_ref)
            n_ref[...] = jnp.zeros_like(n_ref)
            m_ref[...] = jnp.zeros_like(m_ref)
            xpad_ref[:, 0:pad, :] = jnp.zeros((ns, pad, e), F32)

    x = x_ref[...]
    xn = _rmsnorm(x, ng_ref[...]).astype(BF16)
    xm = _dot(xn, win_ref[:, 0:e])
    z = _dot(xn, win_ref[:, e:2 * e])

    for s in range(ns):
        xpad_ref[s, pad:pad + seg, :] = xm[s * seg:(s + 1) * seg]
    pre = cb_ref[...] + cw_ref[CONV_W - 1:CONV_W, :] * xm
    for d in range(1, CONV_W):
        shifted = jnp.concatenate(
            [xpad_ref[s, pad - d:pad - d + seg, :] for s in range(ns)], axis=0)
        pre = pre + cw_ref[CONV_W - 1 - d:CONV_W - d, :] * shifted
    xc = jax.nn.silu(pre)
    for s in range(ns):
        conv_ref[s] = xpad_ref[s, pad + seg - (CONV_W - 1):pad + seg, :]
    if not has_state:
        xpad_ref[0, 0:pad, :] = xpad_ref[0, seg:seg + pad, :]

    xcb = xc.astype(BF16)
    xmb = xm.astype(BF16)
    nt = e // MXU_DIM

    def blockdiag(a, w_ref):
        return jnp.concatenate(
            [_dot(a[:, j * MXU_DIM:(j + 1) * MXU_DIM], w_ref[j]) for j in range(nt)], axis=1)

    q = blockdiag(xcb, bdq_ref)
    k = blockdiag(xcb, bdk_ref)
    v = blockdiag(xmb, bdv_ref)
    qb, kb, vb = q.astype(BF16), k.astype(BF16), v.astype(BF16)
    gates = (_dot(qb, wg_ref[0:e, :]) + _dot(kb, wg_ref[e:2 * e, :])
             + _dot(vb, wg_ref[2 * e:3 * e, :]) + bg_ref[...])
    lf_all = _log_sigmoid(gates)
    ks = k * (dh ** -0.5)
    ksb = ks.astype(BF16)

    ti = lax.broadcasted_iota(jnp.int32, (seg, seg), 0)
    si = lax.broadcasted_iota(jnp.int32, (seg, seg), 1)
    causal = si <= ti
    diag = si == ti
    neg_inf = jnp.float32(-jnp.inf)

    for s in range(ns):
        r0, r1 = s * seg, (s + 1) * seg
        for h in range(HEADS):
            c0, c1 = h * dh, (h + 1) * dh
            ig_c = gates[r0:r1, h:h + 1]
            lf_c = lf_all[r0:r1, HEADS + h:HEADS + h + 1]
            ig_r = jnp.sum(jnp.where(diag, ig_c, 0.0), axis=0, keepdims=True)
            lf_r = jnp.sum(jnp.where(diag, lf_c, 0.0), axis=0, keepdims=True)
            b_r = jnp.sum(jnp.where(ti <= si, lf_c, 0.0), axis=0, keepdims=True)
            b_c = jnp.sum(jnp.where(causal, lf_r, 0.0), axis=1, keepdims=True)

            m_prev = min_ref[s, 0:1, h:h + 1]
            log_d = jnp.where(causal, b_c - b_r + ig_r, neg_inf)
            m_inter = b_c + m_prev
            m_t = jnp.maximum(m_inter, jnp.max(log_d, axis=1, keepdims=True))
            w_intra = jnp.exp(log_d - m_t)
            w_inter = jnp.exp(m_inter - m_t)

            qh = qb[r0:r1, c0:c1]
            vh = vb[r0:r1, c0:c1]
            sc = lax.dot_general(qh, ksb[r0:r1, c0:c1], (((1,), (1,)), ((), ())),
                                 preferred_element_type=F32) * w_intra
            c_prev = cin_ref[s, h]
            n_prev = nin_ref[s, h:h + 1, :]
            num = _dot(sc.astype(BF16), vh) + w_inter * _dot(qh, c_prev.astype(BF16))
            qn = jnp.sum(q[r0:r1, c0:c1] * n_prev, axis=1, keepdims=True)
            den = jnp.sum(sc, axis=1, keepdims=True) + w_inter * qn
            hh = num / jnp.maximum(jnp.abs(den), jnp.exp(-m_t))
            h_ref[r0:r1, c0:c1] = _layernorm(hh, hng_ref[:, c0:c1])

            b_last = b_c[seg - 1:seg, :]
            log_w = b_last - b_c + ig_c
            m_new = jnp.maximum(b_last + m_prev, jnp.max(log_w, axis=0, keepdims=True))
            w = jnp.exp(log_w - m_new)
            decay = jnp.exp(b_last + m_prev - m_new)
            kw = ks[r0:r1, c0:c1] * w
            upd = lax.dot_general(kw.astype(BF16), vh, (((0,), (0,)), ((), ())),
                                  preferred_element_type=F32)
            c_ref[s, h] = decay * c_prev + upd
            n_ref[s, h:h + 1, :] = decay * n_prev + jnp.sum(kw, axis=0, keepdims=True)
            m_ref[s, 0:1, h:h + 1] = m_new

    out = (h_ref[...] + skip_ref[...] * xc) * jax.nn.silu(z)
    y = x + _dot(out.astype(BF16), wout_ref[...])
    y_ref[...] = _rmsnorm(y, fg_ref[...])


def _mlstm_layer(x, weights, state, *, groups, steps, ns, seg):
    e, dh = D_INNER, HEAD_DIM
    rows = ns * seg
    has_state = state is not None
    assert not has_state or steps == 1
    assert has_state or ns == 1
    n_streams = groups * ns
    (norm_g, final_g, w_in, conv_w, conv_b, bdq, bdk, bdv, wg, bg, hn_g, skip, w_out) = weights
    nt = e // MXU_DIM

    row_spec = pl.BlockSpec((rows, D_MODEL), lambda g, t: (g * steps + t, 0))
    c_spec = pl.BlockSpec((ns, HEADS, dh, dh), lambda g, t: (g, 0, 0, 0))
    n_spec = pl.BlockSpec((ns, HEADS, dh), lambda g, t: (g, 0, 0))
    m_spec = pl.BlockSpec((ns, 1, LANES), lambda g, t: (g, 0, 0))
    conv_spec = pl.BlockSpec((ns, CONV_W - 1, e), lambda g, t: (g, 0, 0))
    in_specs = [row_spec,
                _const_spec((1, D_MODEL)), _const_spec((1, D_MODEL)),
                _const_spec((D_MODEL, 2 * e)),
                _const_spec((CONV_W, e)), _const_spec((1, e)),
                _const_spec((nt, MXU_DIM, MXU_DIM)), _const_spec((nt, MXU_DIM, MXU_DIM)),
                _const_spec((nt, MXU_DIM, MXU_DIM)),
                _const_spec((3 * e, LANES)), _const_spec((1, LANES)),
                _const_spec((1, e)), _const_spec((1, e)),
                _const_spec((e, D_MODEL))]
    args = [x, norm_g, final_g, w_in, conv_w, conv_b, bdq, bdk, bdv, wg, bg, hn_g, skip, w_out]
    if has_state:
        in_specs += [c_spec, n_spec, m_spec, conv_spec]
        args += list(state)
    return pl.pallas_call(
        functools.partial(_mlstm_kernel, ns=ns, seg=seg, has_state=has_state),
        grid=(groups, steps),
        in_specs=in_specs,
        out_specs=[row_spec, c_spec, n_spec, m_spec, conv_spec],
        out_shape=[jax.ShapeDtypeStruct((groups * steps * rows, D_MODEL), F32),
                   jax.ShapeDtypeStruct((n_streams, HEADS, dh, dh), F32),
                   jax.ShapeDtypeStruct((n_streams, HEADS, dh), F32),
                   jax.ShapeDtypeStruct((n_streams, 1, LANES), F32),
                   jax.ShapeDtypeStruct((n_streams, CONV_W - 1, e), F32)],
        scratch_shapes=[pltpu.VMEM((ns, SUBLANES + seg, e), F32),
                        pltpu.VMEM((rows, e), F32)],
        compiler_params=pltpu.CompilerParams(
            dimension_semantics=("arbitrary", "arbitrary"),
            vmem_limit_bytes=VMEM_LIMIT_BYTES),
        name="mlstm_layer_state" if has_state else "mlstm_layer",
    )(*args)


def _expand_blockdiag(w):
    nt = D_INNER // MXU_DIM
    per = MXU_DIM // QKV_BLOCK
    w = w.reshape(nt, per, QKV_BLOCK, QKV_BLOCK)
    eye = jnp.eye(per, dtype=w.dtype)
    full = jnp.einsum('jaio,ab->jaibo', w, eye)
    return full.reshape(nt, MXU_DIM, MXU_DIM).astype(BF16)


def kernel(x_prompt, x_sample, state_mlstm_C, state_mlstm_n, state_mlstm_m, state_mlstm_conv, norm_g, final_norm_g, a_w_in, a_ln_g, a_ln_b, a_w_s, a_b_s, a_w_out, b_w_in, b_conv_w, b_conv_b, b_wq, b_wk, b_wv, b_w_gates, b_b_gates, b_hnorm_g, b_skip, b_w_out):
    bsz, seq, d = x_prompt.shape
    dec_b, dec_t, _ = x_sample.shape
    e = D_INNER
    assert norm_g.shape[0] == 2 and a_w_in.shape[0] == 1 and b_w_in.shape[0] == 1
    assert d == D_MODEL and seq % MXU_DIM == 0 and SGU_CHUNK % dec_t == 0

    xp = x_prompt.reshape(bsz * seq, d)
    xs = x_sample.reshape(dec_b * dec_t, d)

    a_in = a_w_in[0].astype(BF16)
    a_out = a_w_out[0].astype(BF16)
    ng0 = norm_g[0].reshape(1, d)
    lng = a_ln_g[0].reshape(1, e)
    lnb = a_ln_b[0].reshape(1, e)
    (xp1,) = _sgu_layer(xp, ng0, a_in, lng, lnb, a_w_s[0], a_b_s[0].T, a_out,
                        tm=MXU_DIM, want_v=False)
    per_tile = SGU_CHUNK // dec_t
    w_head = a_w_s[0][:, :dec_t, :dec_t]
    ws_s = jnp.einsum('ab,gts->gatbs', jnp.eye(per_tile, dtype=F32), w_head)
    ws_s = ws_s.reshape(SGU_GROUPS, SGU_CHUNK, SGU_CHUNK)
    bs_s = jnp.tile(a_b_s[0][:, :dec_t], (1, per_tile)).T
    xs1, v_s = _sgu_layer(xs, ng0, a_in, lng, lnb, ws_s, bs_s, a_out,
                          tm=SGU_CHUNK, want_v=True)

    wg = jnp.pad(b_w_gates[0], ((0, 0), (0, LANES - 2 * HEADS))).astype(BF16)
    bg = jnp.pad(b_b_gates[0], (0, LANES - 2 * HEADS)).reshape(1, LANES)
    weights = (norm_g[1].reshape(1, d), final_norm_g.reshape(1, d), b_w_in[0].astype(BF16),
               b_conv_w[0], b_conv_b[0].reshape(1, e),
               _expand_blockdiag(b_wq[0]), _expand_blockdiag(b_wk[0]),
               _expand_blockdiag(b_wv[0]), wg, bg,
               b_hnorm_g[0].reshape(1, e), b_skip[0].reshape(1, e), b_w_out[0].astype(BF16))
    yp, c_p, n_p, m_p, conv_p = _mlstm_layer(
        xp1, weights, None, groups=bsz, steps=seq // MXU_DIM, ns=1, seg=MXU_DIM)
    m0 = jnp.pad(state_mlstm_m[0], ((0, 0), (0, LANES - HEADS))).reshape(dec_b, 1, LANES)
    ns_s = 2
    ys, c_s, n_s, m_s, conv_s = _mlstm_layer(
        xs1, weights, (state_mlstm_C[0], state_mlstm_n[0], m0, state_mlstm_conv[0]),
        groups=dec_b // ns_s, steps=1, ns=ns_s, seg=dec_t)

    return (yp.reshape(bsz, seq, d),
            ys.reshape(dec_b, dec_t, d),
            v_s.reshape(1, dec_b, dec_t, e),
            c_p[None], n_p[None], m_p[:, 0, :HEADS][None], conv_p[None],
            c_s[None], n_s[None], m_s[:, 0, :HEADS][None], conv_s[None])
```

```python
import functools
import math

import jax
import jax.numpy as jnp
from jax import lax
from jax.experimental import pallas as pl
from jax.experimental.pallas import tpu as pltpu

D_MODEL = 1024
D_INNER = 2048
SGU_CHUNK = 128
SGU_GROUPS = 8
SGU_GROUP_DIM = D_INNER // SGU_GROUPS
HEADS = 4
HEAD_DIM = D_INNER // HEADS
QKV_BLOCK = 4
CONV_W = 4
RMS_EPS = 1e-6
LN_EPS = 1e-5

LANES = 128
SUBLANES = 8
MXU_DIM = 256
VMEM_LIMIT_BYTES = 60 * 1024 * 1024

BF16 = jnp.bfloat16
F32 = jnp.float32
LOG2E = math.log2(math.e)

assert SGU_GROUP_DIM == MXU_DIM


def _dot(a, b):
    return jnp.dot(a, b, preferred_element_type=F32)


def _rmsnorm(x, g):
    return x * lax.rsqrt(jnp.mean(x * x, axis=-1, keepdims=True) + RMS_EPS) * g


def _log_sigmoid(x):
    return jnp.minimum(x, 0.0) - jnp.log1p(jnp.exp(-jnp.abs(x)))


def _prefix_rows(x, op, fill):
    n = x.shape[0]
    row = lax.broadcasted_iota(jnp.int32, x.shape, 0)
    shift = 1
    while shift < n:
        if shift % SUBLANES == 0:
            shifted = jnp.concatenate(
                [jnp.full((shift, x.shape[1]), fill, x.dtype), x[:n - shift]], axis=0)
        else:
            shifted = jnp.where(row >= shift, pltpu.roll(x, shift, 0), fill)
        x = op(x, shifted)
        shift *= 2
    return x


def _const_spec(shape):
    zeros = (0,) * len(shape)
    return pl.BlockSpec(shape, lambda *_: zeros, pipeline_mode=pl.Buffered(1))


def _col_slabs(w):
    k, n = w.shape
    return w.reshape(k, n // MXU_DIM, MXU_DIM).transpose(1, 0, 2).astype(BF16)


def _sgu_kernel(x_ref, ng_ref, win_ref, lng_ref, lnb_ref, ws_ref, bs_ref, wout_ref,
                y_ref, *v_refs, tm):
    ng = SGU_GROUPS
    e = D_INNER
    x = x_ref[...]
    xn = _rmsnorm(x, ng_ref[...]).astype(BF16)

    v_g, uz_g = [], []
    row_sum = jnp.zeros((tm, 1), F32)
    for g in range(ng):
        v = jax.nn.gelu(_dot(xn, win_ref[ng + g]))
        row_sum = row_sum + jnp.sum(v, axis=1, keepdims=True)
        v_g.append(v)
    mu = row_sum * (1.0 / e)
    sq_sum = jnp.zeros((tm, 1), F32)
    for g in range(ng):
        u = jax.nn.gelu(_dot(xn, win_ref[g]))
        z = _dot(xn, win_ref[2 * ng + g])
        uz_g.append(u * jax.nn.silu(z))
        v_g[g] = v_g[g] - mu
        sq_sum = sq_sum + jnp.sum(v_g[g] * v_g[g], axis=1, keepdims=True)
    rstd = lax.rsqrt(sq_sum * (1.0 / e) + LN_EPS)

    ti = lax.broadcasted_iota(jnp.int32, (SGU_CHUNK, SGU_CHUNK), 0)
    si = lax.broadcasted_iota(jnp.int32, (SGU_CHUNK, SGU_CHUNK), 1)
    causal = si <= ti
    s_g = []
    for g in range(ng):
        lo, hi = g * SGU_GROUP_DIM, (g + 1) * SGU_GROUP_DIM
        vn = v_g[g] * rstd * lng_ref[:, lo:hi] + lnb_ref[:, lo:hi]
        if v_refs:
            v_refs[0][:, lo:hi] = vn
        vb = vn.astype(BF16)
        wm = jnp.where(causal, ws_ref[g], 0.0).astype(BF16)
        bias = bs_ref[:, g:g + 1]
        s_g.append(jnp.concatenate(
            [_dot(wm, vb[c * SGU_CHUNK:(c + 1) * SGU_CHUNK]) + bias
             for c in range(tm // SGU_CHUNK)], axis=0))
    n_out = D_MODEL // MXU_DIM
    acc = [None] * n_out
    for g in range(ng):
        lo, hi = g * SGU_GROUP_DIM, (g + 1) * SGU_GROUP_DIM
        out = (uz_g[g] * s_g[g]).astype(BF16)
        for n in range(n_out):
            part = _dot(out, wout_ref[n, lo:hi, :])
            acc[n] = part if acc[n] is None else acc[n] + part
    y_ref[...] = x + jnp.concatenate(acc, axis=1)


def _sgu_layer(x, norm_g, w_in, ln_g, ln_b, w_s, b_s, w_out, *, tm, want_v):
    rows = x.shape[0]
    e = D_INNER
    row_spec = pl.BlockSpec((tm, D_MODEL), lambda i: (i, 0))
    out_shape = [jax.ShapeDtypeStruct((rows, D_MODEL), F32)]
    out_specs = [row_spec]
    if want_v:
        out_shape.append(jax.ShapeDtypeStruct((rows, e), F32))
        out_specs.append(pl.BlockSpec((tm, e), lambda i: (i, 0)))
    res = pl.pallas_call(
        functools.partial(_sgu_kernel, tm=tm),
        grid=(rows // tm,),
        in_specs=[row_spec,
                  _const_spec((1, D_MODEL)),
                  _const_spec(w_in.shape),
                  _const_spec((1, e)),
                  _const_spec((1, e)),
                  _const_spec((SGU_GROUPS, SGU_CHUNK, SGU_CHUNK)),
                  _const_spec((SGU_CHUNK, SGU_GROUPS)),
                  _const_spec(w_out.shape)],
        out_specs=out_specs,
        out_shape=out_shape,
        compiler_params=pltpu.CompilerParams(
            dimension_semantics=("arbitrary",), vmem_limit_bytes=VMEM_LIMIT_BYTES),
        name="sgu_layer_v" if want_v else "sgu_layer",
    )(x, norm_g, w_in, ln_g, ln_b, w_s, b_s, w_out)
    return res


def _cols_to_rows(a, seg):
    if seg % LANES == 0:
        return a.T[0:SUBLANES, :]
    ti = lax.broadcasted_iota(jnp.int32, (seg, seg), 0)
    si = lax.broadcasted_iota(jnp.int32, (seg, seg), 1)
    return jnp.concatenate(
        [jnp.sum(jnp.where(ti == si, a[:, h:h + 1], 0.0), axis=0, keepdims=True)
         for h in range(HEADS)], axis=0)


def _mlstm_kernel(*refs, ns, seg, has_state):
    (x_ref, ng_ref, fg_ref, win_ref, cw_ref, cb_ref, bdq_ref, bdk_ref, bdv_ref, wg_ref,
     bg_ref, hng_ref, skip_ref, wout_ref) = refs[:14]
    rest = refs[14:]
    if has_state:
        c0_ref, n0_ref, m0_ref, conv0_ref = rest[:4]
        rest = rest[4:]
    y_ref, c_ref, n_ref, m_ref, conv_ref, xpad_ref = rest
    e, dh = D_INNER, HEAD_DIM
    pad = SUBLANES
    tiles_per_head = dh // MXU_DIM
    n_out = D_MODEL // MXU_DIM

    if has_state:
        cin_ref, nin_ref, min_ref = c0_ref, n0_ref, m0_ref
        xpad_ref[:, pad - (CONV_W - 1):pad, :] = conv0_ref[...]
    else:
        cin_ref, nin_ref, min_ref = c_ref, n_ref, m_ref

        @pl.when(pl.program_id(1) == 0)
        def _():
            c_ref[...] = jnp.zeros_like(c_ref)
            n_ref[...] = jnp.zeros_like(n_ref)
            m_ref[...] = jnp.zeros_like(m_ref)
            xpad_ref[:, 0:pad, :] = jnp.zeros((ns, pad, e), F32)

    x = x_ref[...]
    xn = _rmsnorm(x, ng_ref[...]).astype(BF16)

    xc_h, z_h, q_h, qb_h, ks_h, vb_h = [], [], [], [], [], []
    gates = bg_ref[...]

    def project(h):
        t0 = h * tiles_per_head
        xm_ = jnp.concatenate(
            [_dot(xn, win_ref[t0 + j]) for j in range(tiles_per_head)], axis=1)
        z_ = jnp.concatenate(
            [_dot(xn, win_ref[e // MXU_DIM + t0 + j]) for j in range(tiles_per_head)], axis=1)
        return xm_, z_

    projected = project(0)
    for h in range(HEADS):
        c0, c1 = h * dh, (h + 1) * dh
        t0 = h * tiles_per_head
        xm, z = projected
        z_h.append(z)
        if h + 1 < HEADS:
            projected = project(h + 1)
        for s in range(ns):
            xpad_ref[s, pad:pad + seg, c0:c1] = xm[s * seg:(s + 1) * seg]
        pre = cb_ref[:, c0:c1] + cw_ref[CONV_W - 1:CONV_W, c0:c1] * xm
        for d in range(1, CONV_W):
            shifted = jnp.concatenate(
                [xpad_ref[s, pad - d:pad - d + seg, c0:c1] for s in range(ns)], axis=0)
            pre = pre + cw_ref[CONV_W - 1 - d:CONV_W - d, c0:c1] * shifted
        xc = jax.nn.silu(pre)
        xc_h.append(xc)
        xcb = xc.astype(BF16)
        xmb = xm.astype(BF16)

        def blockdiag(a, w_ref):
            return jnp.concatenate(
                [_dot(a[:, j * MXU_DIM:(j + 1) * MXU_DIM], w_ref[t0 + j])
                 for j in range(tiles_per_head)], axis=1)

        q = blockdiag(xcb, bdq_ref)
        k = blockdiag(xcb, bdk_ref)
        v = blockdiag(xmb, bdv_ref)
        qb, kb, vb = q.astype(BF16), k.astype(BF16), v.astype(BF16)
        gates = (gates + _dot(qb, wg_ref[c0:c1, :]) + _dot(kb, wg_ref[e + c0:e + c1, :])
                 + _dot(vb, wg_ref[2 * e + c0:2 * e + c1, :]))
        q_h.append(q)
        qb_h.append(qb)
        ks_h.append(k * (dh ** -0.5))
        vb_h.append(vb)

    for s in range(ns):
        conv_ref[s] = xpad_ref[s, pad + seg - (CONV_W - 1):pad + seg, :]
    if not has_state:
        xpad_ref[0, 0:pad, :] = xpad_ref[0, seg:seg + pad, :]

    ti = lax.broadcasted_iota(jnp.int32, (seg, seg), 0)
    si = lax.broadcasted_iota(jnp.int32, (seg, seg), 1)
    causal = si <= ti

    y_rows = []
    for s in range(ns):
        r0, r1 = s * seg, (s + 1) * seg
        ig = gates[r0:r1, 0:LANES]
        lf = _log_sigmoid(gates[r0:r1, LANES:2 * LANES])
        b = _prefix_rows(lf, jnp.add, 0.0)
        a = ig - b
        cm = _prefix_rows(a, jnp.maximum, -jnp.inf)
        m_prev = min_ref[s]
        neg_mt_plus_b = -jnp.maximum(m_prev, cm)
        w_inter_all = jnp.exp(m_prev + neg_mt_plus_b)
        inv_floor_all = jnp.exp(neg_mt_plus_b - b)
        b_last = b[seg - 1:seg, :]
        m_new = b_last + jnp.maximum(m_prev, cm[seg - 1:seg, :])
        w_all = jnp.exp(b_last + a - m_new)
        decay_all = jnp.exp(b_last + m_prev - m_new)
        m_ref[s] = m_new
        a_rows = _cols_to_rows(a, seg) * LOG2E
        u_all = neg_mt_plus_b * LOG2E
        n_prev_all = nin_ref[s]

        sc_h, qc_h, cprev_h = [], [], []
        for h in range(HEADS):
            qh = qb_h[h][r0:r1]
            w_intra = jnp.where(causal, jnp.exp2(u_all[:, h:h + 1] + a_rows[h:h + 1, :]), 0.0)
            sc_h.append(lax.dot_general(qh, ks_h[h][r0:r1].astype(BF16),
                                        (((1,), (1,)), ((), ())),
                                        preferred_element_type=F32) * w_intra)
            c_prev = cin_ref[s, h]
            cprev_h.append(c_prev)
            qc_h.append(_dot(qh, c_prev.astype(BF16)))

        num_h = [_dot(sc_h[h].astype(BF16), vb_h[h][r0:r1])
                 + w_inter_all[:, h:h + 1] * qc_h[h] for h in range(HEADS)]

        out_h, n_new = [], []
        for h in range(HEADS):
            c0, c1 = h * dh, (h + 1) * dh
            num = num_h[h]
            qn = jnp.sum(q_h[h][r0:r1] * n_prev_all[h:h + 1, :], axis=1, keepdims=True)
            den = jnp.sum(sc_h[h], axis=1, keepdims=True) + w_inter_all[:, h:h + 1] * qn
            r = 1.0 / jnp.maximum(jnp.abs(den), inv_floor_all[:, h:h + 1])
            cen = num - jnp.mean(num, axis=1, keepdims=True)
            var = jnp.mean(cen * cen, axis=1, keepdims=True)
            hn = cen * (r * lax.rsqrt(r * r * var + LN_EPS)) * hng_ref[:, c0:c1]
            out_h.append(((hn + skip_ref[:, c0:c1] * xc_h[h][r0:r1])
                          * jax.nn.silu(z_h[h][r0:r1])).astype(BF16))

            decay = decay_all[:, h:h + 1]
            kw = ks_h[h][r0:r1] * w_all[:, h:h + 1]
            upd = lax.dot_general(kw.astype(BF16), vb_h[h][r0:r1], (((0,), (0,)), ((), ())),
                                  preferred_element_type=F32)
            c_ref[s, h] = decay * cprev_h[h] + upd
            n_new.append(decay * n_prev_all[h:h + 1, :] + jnp.sum(kw, axis=0, keepdims=True))
        n_ref[s] = jnp.concatenate(n_new, axis=0)

        acc = [None] * n_out
        for h in range(HEADS):
            for n in range(n_out):
                part = _dot(out_h[h], wout_ref[n, h * dh:(h + 1) * dh, :])
                acc[n] = part if acc[n] is None else acc[n] + part
        y_rows.append(jnp.concatenate(acc, axis=1))

    y = x + (y_rows[0] if ns == 1 else jnp.concatenate(y_rows, axis=0))
    y_ref[...] = _rmsnorm(y, fg_ref[...])


def _mlstm_layer(x, weights, state, *, groups, steps, ns, seg):
    e, dh = D_INNER, HEAD_DIM
    rows = ns * seg
    has_state = state is not None
    assert not has_state or steps == 1
    assert has_state or ns == 1
    n_streams = groups * ns
    (norm_g, final_g, w_in, conv_w, conv_b, bdq, bdk, bdv, wg, bg, hn_g, skip, w_out) = weights
    nt = e // MXU_DIM

    row_spec = pl.BlockSpec((rows, D_MODEL), lambda g, t: (g * steps + t, 0))
    c_spec = pl.BlockSpec((ns, HEADS, dh, dh), lambda g, t: (g, 0, 0, 0))
    n_spec = pl.BlockSpec((ns, HEADS, dh), lambda g, t: (g, 0, 0))
    m_spec = pl.BlockSpec((ns, 1, LANES), lambda g, t: (g, 0, 0))
    conv_spec = pl.BlockSpec((ns, CONV_W - 1, e), lambda g, t: (g, 0, 0))
    in_specs = [row_spec,
                _const_spec((1, D_MODEL)), _const_spec((1, D_MODEL)),
                _const_spec(w_in.shape),
                _const_spec((CONV_W, e)), _const_spec((1, e)),
                _const_spec((nt, MXU_DIM, MXU_DIM)), _const_spec((nt, MXU_DIM, MXU_DIM)),
                _const_spec((nt, MXU_DIM, MXU_DIM)),
                _const_spec((3 * e, 2 * LANES)), _const_spec((1, 2 * LANES)),
                _const_spec((1, e)), _const_spec((1, e)),
                _const_spec(w_out.shape)]
    args = [x, norm_g, final_g, w_in, conv_w, conv_b, bdq, bdk, bdv, wg, bg, hn_g, skip, w_out]
    if has_state:
        in_specs += [c_spec, n_spec, m_spec, conv_spec]
        args += list(state)
    return pl.pallas_call(
        functools.partial(_mlstm_kernel, ns=ns, seg=seg, has_state=has_state),
        grid=(groups, steps),
        in_specs=in_specs,
        out_specs=[row_spec, c_spec, n_spec, m_spec, conv_spec],
        out_shape=[jax.ShapeDtypeStruct((groups * steps * rows, D_MODEL), F32),
                   jax.ShapeDtypeStruct((n_streams, HEADS, dh, dh), F32),
                   jax.ShapeDtypeStruct((n_streams, HEADS, dh), F32),
                   jax.ShapeDtypeStruct((n_streams, 1, LANES), F32),
                   jax.ShapeDtypeStruct((n_streams, CONV_W - 1, e), F32)],
        scratch_shapes=[pltpu.VMEM((ns, SUBLANES + seg, e), F32)],
        compiler_params=pltpu.CompilerParams(
            dimension_semantics=("arbitrary", "arbitrary"),
            vmem_limit_bytes=VMEM_LIMIT_BYTES),
        name="mlstm_layer_state" if has_state else "mlstm_layer",
    )(*args)


def _expand_blockdiag(w):
    nt = D_INNER // MXU_DIM
    rows = w.reshape(nt, MXU_DIM, QKV_BLOCK)
    tiled = jnp.tile(rows, (1, 1, MXU_DIM // QKV_BLOCK))
    r = lax.broadcasted_iota(jnp.int32, (MXU_DIM, MXU_DIM), 0) // QKV_BLOCK
    c = lax.broadcasted_iota(jnp.int32, (MXU_DIM, MXU_DIM), 1) // QKV_BLOCK
    return jnp.where(r == c, tiled, 0.0).astype(BF16)


def _head_lanes(w):
    lead = w.shape[:-1]
    z = jnp.zeros(lead + (LANES - HEADS,), w.dtype)
    return jnp.concatenate([w[..., :HEADS], z, w[..., HEADS:], z], axis=-1)


def kernel(x_prompt, x_sample, state_mlstm_C, state_mlstm_n, state_mlstm_m, state_mlstm_conv, norm_g, final_norm_g, a_w_in, a_ln_g, a_ln_b, a_w_s, a_b_s, a_w_out, b_w_in, b_conv_w, b_conv_b, b_wq, b_wk, b_wv, b_w_gates, b_b_gates, b_hnorm_g, b_skip, b_w_out):
    bsz, seq, d = x_prompt.shape
    dec_b, dec_t, _ = x_sample.shape
    e = D_INNER
    assert norm_g.shape[0] == 2 and a_w_in.shape[0] == 1 and b_w_in.shape[0] == 1
    assert d == D_MODEL and seq % MXU_DIM == 0 and SGU_CHUNK % dec_t == 0

    xp = x_prompt.reshape(bsz * seq, d)
    xs = x_sample.reshape(dec_b * dec_t, d)

    a_in = _col_slabs(a_w_in[0])
    a_out = _col_slabs(a_w_out[0])
    ng0 = norm_g[0].reshape(1, d)
    lng = a_ln_g[0].reshape(1, e)
    lnb = a_ln_b[0].reshape(1, e)
    (xp1,) = _sgu_layer(xp, ng0, a_in, lng, lnb, a_w_s[0], a_b_s[0].T, a_out,
                        tm=MXU_DIM, want_v=False)
    per_tile = SGU_CHUNK // dec_t
    w_head = a_w_s[0][:, :dec_t, :dec_t]
    ws_s = jnp.einsum('ab,gts->gatbs', jnp.eye(per_tile, dtype=F32), w_head)
    ws_s = ws_s.reshape(SGU_GROUPS, SGU_CHUNK, SGU_CHUNK)
    bs_s = jnp.tile(a_b_s[0][:, :dec_t], (1, per_tile)).T
    xs1, v_s = _sgu_layer(xs, ng0, a_in, lng, lnb, ws_s, bs_s, a_out,
                          tm=SGU_CHUNK, want_v=True)

    wg = _head_lanes(b_w_gates[0]).astype(BF16)
    bg = _head_lanes(b_b_gates[0]).reshape(1, 2 * LANES)
    weights = (norm_g[1].reshape(1, d), final_norm_g.reshape(1, d), _col_slabs(b_w_in[0]),
               b_conv_w[0], b_conv_b[0].reshape(1, e),
               _expand_blockdiag(b_wq[0]), _expand_blockdiag(b_wk[0]),
               _expand_blockdiag(b_wv[0]), wg, bg,
               b_hnorm_g[0].reshape(1, e), b_skip[0].reshape(1, e), _col_slabs(b_w_out[0]))
    yp, c_p, n_p, m_p, conv_p = _mlstm_layer(
        xp1, weights, None, groups=bsz, steps=seq // MXU_DIM, ns=1, seg=MXU_DIM)
    m0 = jnp.pad(state_mlstm_m[0], ((0, 0), (0, LANES - HEADS))).reshape(dec_b, 1, LANES)
    ns_s = 2
    ys, c_s, n_s, m_s, conv_s = _mlstm_layer(
        xs1, weights, (state_mlstm_C[0], state_mlstm_n[0], m0, state_mlstm_conv[0]),
        groups=dec_b // ns_s, steps=1, ns=ns_s, seg=dec_t)

    return (yp.reshape(bsz, seq, d),
            ys.reshape(dec_b, dec_t, d),
            v_s.reshape(1, dec_b, dec_t, e),
            c_p[None], n_p[None], m_p[:, 0, :HEADS][None], conv_p[None],
            c_s[None], n_s[None], m_s[:, 0, :HEADS][None], conv_s[None])
```

```python
import functools
import math

import numpy as np
import jax
import jax.numpy as jnp
from jax import lax
from jax.experimental import pallas as pl
from jax.experimental.pallas import tpu as pltpu

D_MODEL = 1024
D_INNER = 2048
SGU_CHUNK = 128
SGU_GROUPS = 8
SGU_GROUP_DIM = D_INNER // SGU_GROUPS
HEADS = 4
HEAD_DIM = D_INNER // HEADS
QKV_BLOCK = 4
CONV_W = 4
RMS_EPS = 1e-6
LN_EPS = 1e-5

LANES = 128
SUBLANES = 8
MXU_DIM = 256
VMEM_LIMIT_BYTES = 60 * 1024 * 1024

BF16 = jnp.bfloat16
F32 = jnp.float32
LOG2E = math.log2(math.e)

assert SGU_GROUP_DIM == MXU_DIM


def _dot(a, b):
    return jnp.dot(a, b, preferred_element_type=F32)


def _rmsnorm(x, g):
    return x * lax.rsqrt(jnp.mean(x * x, axis=-1, keepdims=True) + RMS_EPS) * g


def _log_sigmoid(x):
    return jnp.minimum(x, 0.0) - jnp.log1p(jnp.exp(-jnp.abs(x)))


def _prefix_rows(x, op, fill):
    n = x.shape[0]
    row = lax.broadcasted_iota(jnp.int32, x.shape, 0)
    shift = 1
    while shift < n:
        if shift % SUBLANES == 0:
            shifted = jnp.concatenate(
                [jnp.full((shift, x.shape[1]), fill, x.dtype), x[:n - shift]], axis=0)
        else:
            shifted = jnp.where(row >= shift, pltpu.roll(x, shift, 0), fill)
        x = op(x, shifted)
        shift *= 2
    return x


def _const_spec(shape):
    zeros = (0,) * len(shape)
    return pl.BlockSpec(shape, lambda *_: zeros, pipeline_mode=pl.Buffered(1))


def _slab_kernel(w_ref, o_ref):
    for j in range(o_ref.shape[0]):
        o_ref[j] = w_ref[:, j * MXU_DIM:(j + 1) * MXU_DIM].astype(BF16)


def _col_slabs(w):
    k, n = w.shape
    rows = MXU_DIM
    return pl.pallas_call(
        _slab_kernel,
        grid=(k // rows,),
        in_specs=[pl.BlockSpec((rows, n), lambda i: (i, 0))],
        out_specs=pl.BlockSpec((n // MXU_DIM, rows, MXU_DIM), lambda i: (0, i, 0)),
        out_shape=jax.ShapeDtypeStruct((n // MXU_DIM, k, MXU_DIM), BF16),
        compiler_params=pltpu.CompilerParams(dimension_semantics=("arbitrary",)),
        name="weight_slabs",
    )(w)


def _sgu_kernel(x_ref, ng_ref, win_ref, lng_ref, lnb_ref, ws_ref, bs_ref, wout_ref,
                y_ref, *v_refs, tm):
    ng = SGU_GROUPS
    e = D_INNER
    x = x_ref[...]
    xn = _rmsnorm(x, ng_ref[...]).astype(BF16)

    v_g, uz_g = [], []
    row_sum = jnp.zeros((tm, 1), F32)
    for g in range(ng):
        v = jax.nn.gelu(_dot(xn, win_ref[ng + g]))
        row_sum = row_sum + jnp.sum(v, axis=1, keepdims=True)
        v_g.append(v)
    mu = row_sum * (1.0 / e)
    sq_sum = jnp.zeros((tm, 1), F32)
    for g in range(ng):
        u = jax.nn.gelu(_dot(xn, win_ref[g]))
        z = _dot(xn, win_ref[2 * ng + g])
        uz_g.append(u * jax.nn.silu(z))
        v_g[g] = v_g[g] - mu
        sq_sum = sq_sum + jnp.sum(v_g[g] * v_g[g], axis=1, keepdims=True)
    rstd = lax.rsqrt(sq_sum * (1.0 / e) + LN_EPS)

    ti = lax.broadcasted_iota(jnp.int32, (SGU_CHUNK, SGU_CHUNK), 0)
    si = lax.broadcasted_iota(jnp.int32, (SGU_CHUNK, SGU_CHUNK), 1)
    causal = si <= ti
    s_g = []
    for g in range(ng):
        lo, hi = g * SGU_GROUP_DIM, (g + 1) * SGU_GROUP_DIM
        vn = v_g[g] * rstd * lng_ref[:, lo:hi] + lnb_ref[:, lo:hi]
        if v_refs:
            v_refs[0][:, lo:hi] = vn
        vb = vn.astype(BF16)
        wm = jnp.where(causal, ws_ref[g], 0.0).astype(BF16)
        bias = bs_ref[:, g:g + 1]
        s_g.append(jnp.concatenate(
            [_dot(wm, vb[c * SGU_CHUNK:(c + 1) * SGU_CHUNK]) + bias
             for c in range(tm // SGU_CHUNK)], axis=0))
    n_out = D_MODEL // MXU_DIM
    acc = [None] * n_out
    for g in range(ng):
        lo, hi = g * SGU_GROUP_DIM, (g + 1) * SGU_GROUP_DIM
        out = (uz_g[g] * s_g[g]).astype(BF16)
        for n in range(n_out):
            part = _dot(out, wout_ref[n, lo:hi, :])
            acc[n] = part if acc[n] is None else acc[n] + part
    y_ref[...] = x + jnp.concatenate(acc, axis=1)


def _sgu_layer(x, norm_g, w_in, ln_g, ln_b, w_s, b_s, w_out, *, tm, want_v):
    rows = x.shape[0]
    e = D_INNER
    row_spec = pl.BlockSpec((tm, D_MODEL), lambda i: (i, 0))
    out_shape = [jax.ShapeDtypeStruct((rows, D_MODEL), F32)]
    out_specs = [row_spec]
    if want_v:
        out_shape.append(jax.ShapeDtypeStruct((rows, e), F32))
        out_specs.append(pl.BlockSpec((tm, e), lambda i: (i, 0)))
    res = pl.pallas_call(
        functools.partial(_sgu_kernel, tm=tm),
        grid=(rows // tm,),
        in_specs=[row_spec,
                  _const_spec((1, D_MODEL)),
                  _const_spec(w_in.shape),
                  _const_spec((1, e)),
                  _const_spec((1, e)),
                  _const_spec((SGU_GROUPS, SGU_CHUNK, SGU_CHUNK)),
                  _const_spec((SGU_CHUNK, SGU_GROUPS)),
                  _const_spec(w_out.shape)],
        out_specs=out_specs,
        out_shape=out_shape,
        compiler_params=pltpu.CompilerParams(
            dimension_semantics=("arbitrary",), vmem_limit_bytes=VMEM_LIMIT_BYTES),
        name="sgu_layer_v" if want_v else "sgu_layer",
    )(x, norm_g, w_in, ln_g, ln_b, w_s, b_s, w_out)
    return res


def _cols_to_rows(a, seg):
    if seg % LANES == 0:
        return a.T[0:SUBLANES, :]
    ti = lax.broadcasted_iota(jnp.int32, (seg, seg), 0)
    si = lax.broadcasted_iota(jnp.int32, (seg, seg), 1)
    return jnp.concatenate(
        [jnp.sum(jnp.where(ti == si, a[:, h:h + 1], 0.0), axis=0, keepdims=True)
         for h in range(HEADS)], axis=0)


_HANDOFF = (('xc', D_INNER, F32), ('sz', D_INNER, F32), ('qb', D_INNER, BF16),
            ('ks', D_INNER, F32), ('vb', D_INNER, BF16), ('gates', 2 * LANES, F32))
_HANDOFF_NAMES = tuple(name for name, _, _ in _HANDOFF)


def _interleave(*gens):
    live = list(gens)
    while live:
        for gen in list(live):
            try:
                next(gen)
            except StopIteration:
                live.remove(gen)


def _frontend(x_ref, ng_ref, win_ref, cw_ref, cb_ref, bdq_ref, bdk_ref, bdv_ref, wg_ref,
              bg_ref, xpad_ref, dst, *, ns, seg):
    e, dh = D_INNER, HEAD_DIM
    pad = SUBLANES
    rows = ns * seg
    tiles_per_head = dh // MXU_DIM
    xn = _rmsnorm(x_ref[...], ng_ref[...]).astype(BF16)

    def project(h):
        t0 = h * tiles_per_head
        xm_ = jnp.concatenate(
            [_dot(xn, win_ref[t0 + j]) for j in range(tiles_per_head)], axis=1)
        z_ = jnp.concatenate(
            [_dot(xn, win_ref[e // MXU_DIM + t0 + j]) for j in range(tiles_per_head)], axis=1)
        return xm_, z_

    projected = project(0)
    gates = bg_ref[...]
    yield
    for h in range(HEADS):
        c0, c1 = h * dh, (h + 1) * dh
        t0 = h * tiles_per_head
        xm, z = projected
        if h + 1 < HEADS:
            projected = project(h + 1)
        xpad_ref[:, pad:pad + seg, c0:c1] = xm.reshape(ns, seg, dh)
        pre = cb_ref[:, c0:c1] + cw_ref[CONV_W - 1:CONV_W, c0:c1] * xm
        for d in range(1, CONV_W):
            shifted = xpad_ref[:, pad - d:pad - d + seg, c0:c1].reshape(rows, dh)
            pre = pre + cw_ref[CONV_W - 1 - d:CONV_W - d, c0:c1] * shifted
        xc = jax.nn.silu(pre)
        xcb = xc.astype(BF16)
        xmb = xm.astype(BF16)

        def blockdiag(a, w_ref):
            return jnp.concatenate(
                [_dot(a[:, j * MXU_DIM:(j + 1) * MXU_DIM], w_ref[t0 + j])
                 for j in range(tiles_per_head)], axis=1)

        q = blockdiag(xcb, bdq_ref)
        k = blockdiag(xcb, bdk_ref)
        v = blockdiag(xmb, bdv_ref)
        qb, vb = q.astype(BF16), v.astype(BF16)
        gates = gates + _dot(xcb, wg_ref[0, c0:c1, :]) + _dot(xmb, wg_ref[1, c0:c1, :])
        dst['xc'][:, c0:c1] = xc
        dst['sz'][:, c0:c1] = jax.nn.silu(z)
        dst['qb'][:, c0:c1] = qb
        dst['ks'][:, c0:c1] = k * (dh ** -0.5)
        dst['vb'][:, c0:c1] = vb
        if h == HEADS - 1:
            dst['gates'][...] = gates
        yield


def _backend(xres_ref, src, cin_ref, nin_ref, min_ref, hng_ref, skip_ref, wout_ref, fg_ref,
             y_ref, c_ref, n_ref, m_ref, *, ns, seg):
    dh = HEAD_DIM
    n_out = D_MODEL // MXU_DIM
    ti = lax.broadcasted_iota(jnp.int32, (seg, seg), 0)
    si = lax.broadcasted_iota(jnp.int32, (seg, seg), 1)
    causal = si <= ti
    pairs = [(s, h) for s in range(ns) for h in range(HEADS)]

    def block(name, s, h):
        return src[name][s * seg:(s + 1) * seg, h * dh:(h + 1) * dh]

    qc = {(s, h): _dot(block('qb', s, h), cin_ref[s, h].astype(BF16)) for s, h in pairs}
    gt = []
    for s in range(ns):
        g = src['gates'][s * seg:(s + 1) * seg, :]
        ig = g[:, 0:LANES]
        lf = _log_sigmoid(g[:, LANES:2 * LANES])
        b = _prefix_rows(lf, jnp.add, 0.0)
        a = ig - b
        cm = _prefix_rows(a, jnp.maximum, -jnp.inf)
        m_prev = min_ref[s]
        neg_mt_plus_b = -jnp.maximum(m_prev, cm)
        b_last = b[seg - 1:seg, :]
        m_new = b_last + jnp.maximum(m_prev, cm[seg - 1:seg, :])
        gt.append(dict(
            w_inter=jnp.exp(m_prev + neg_mt_plus_b),
            inv_floor=jnp.exp(neg_mt_plus_b - b),
            w=jnp.exp(b_last + a - m_new),
            decay=jnp.exp(b_last + m_prev - m_new),
            a_rows=_cols_to_rows(a, seg) * LOG2E,
            u=neg_mt_plus_b * LOG2E,
            n_prev=nin_ref[s]))
        m_ref[s] = m_new
    yield

    sc = {}
    for s, h in pairs:
        w_intra = jnp.where(
            causal, jnp.exp2(gt[s]['u'][:, h:h + 1] + gt[s]['a_rows'][h:h + 1, :]), 0.0)
        sc[s, h] = lax.dot_general(block('qb', s, h), block('ks', s, h).astype(BF16),
                                   (((1,), (1,)), ((), ())),
                                   preferred_element_type=F32) * w_intra
    yield

    num = {(s, h): _dot(sc[s, h].astype(BF16), block('vb', s, h))
           + gt[s]['w_inter'][:, h:h + 1] * qc[s, h] for s, h in pairs}
    yield

    out, n_new = {}, {}
    for s, h in pairs:
        c0, c1 = h * dh, (h + 1) * dh
        n_prev = gt[s]['n_prev'][h:h + 1, :]
        qn = jnp.sum(block('qb', s, h).astype(F32) * n_prev, axis=1, keepdims=True)
        den = jnp.sum(sc[s, h], axis=1, keepdims=True) + gt[s]['w_inter'][:, h:h + 1] * qn
        r = 1.0 / jnp.maximum(jnp.abs(den), gt[s]['inv_floor'][:, h:h + 1])
        cen = num[s, h] - jnp.mean(num[s, h], axis=1, keepdims=True)
        var = jnp.mean(cen * cen, axis=1, keepdims=True)
        hn = cen * (r * lax.rsqrt(r * r * var + LN_EPS)) * hng_ref[:, c0:c1]
        out[s, h] = ((hn + skip_ref[:, c0:c1] * block('xc', s, h))
                     * block('sz', s, h)).astype(BF16)

        decay = gt[s]['decay'][:, h:h + 1]
        kw = block('ks', s, h) * gt[s]['w'][:, h:h + 1]
        upd = lax.dot_general(kw.astype(BF16), block('vb', s, h), (((0,), (0,)), ((), ())),
                              preferred_element_type=F32)
        c_ref[s, h] = decay * cin_ref[s, h] + upd
        n_new[s, h] = decay * n_prev + jnp.sum(kw, axis=0, keepdims=True)
    for s in range(ns):
        n_ref[s] = jnp.concatenate([n_new[s, h] for h in range(HEADS)], axis=0)
    yield

    y_rows = []
    for s in range(ns):
        acc = [None] * n_out
        for h in range(HEADS):
            for n in range(n_out):
                part = _dot(out[s, h], wout_ref[n, h * dh:(h + 1) * dh, :])
                acc[n] = part if acc[n] is None else acc[n] + part
        y_rows.append(jnp.concatenate(acc, axis=1))
    y = xres_ref[...] + (y_rows[0] if ns == 1 else jnp.concatenate(y_rows, axis=0))
    y_ref[...] = _rmsnorm(y, fg_ref[...])


def _mlstm_pipelined_kernel(x_ref, xprev_ref, ng_ref, fg_ref, win_ref, cw_ref, cb_ref,
                            bdq_ref, bdk_ref, bdv_ref, wg_ref, bg_ref, hng_ref, skip_ref,
                            wout_ref, y_ref, c_ref, n_ref, m_ref, conv_ref, xpad_ref,
                            *handoff, seg):
    t = pl.program_id(1)
    e = D_INNER
    pad = SUBLANES
    n_hand = len(_HANDOFF)
    sets = (dict(zip(_HANDOFF_NAMES, handoff[:n_hand])),
            dict(zip(_HANDOFF_NAMES, handoff[n_hand:])))

    @pl.when(t == 0)
    def _():
        xpad_ref[:, 0:pad, :] = jnp.zeros((1, pad, e), F32)
        for ref in sets[1].values():
            ref[...] = jnp.zeros(ref.shape, ref.dtype)

    @pl.when(t <= 1)
    def _():
        c_ref[...] = jnp.zeros(c_ref.shape, F32)
        n_ref[...] = jnp.zeros(n_ref.shape, F32)
        m_ref[...] = jnp.zeros(m_ref.shape, F32)

    def step(dst, src):
        front = _frontend(x_ref, ng_ref, win_ref, cw_ref, cb_ref, bdq_ref, bdk_ref, bdv_ref,
                          wg_ref, bg_ref, xpad_ref, dst, ns=1, seg=seg)
        back = _backend(xprev_ref, src, c_ref, n_ref, m_ref, hng_ref, skip_ref, wout_ref,
                        fg_ref, y_ref, c_ref, n_ref, m_ref, ns=1, seg=seg)
        _interleave(back, front)
        conv_ref[...] = xpad_ref[:, pad + seg - (CONV_W - 1):pad + seg, :]
        xpad_ref[:, 0:pad, :] = xpad_ref[:, seg:seg + pad, :]

    @pl.when(t % 2 == 0)
    def _():
        step(sets[0], sets[1])

    @pl.when(t % 2 == 1)
    def _():
        step(sets[1], sets[0])


def _mlstm_front_kernel(x_ref, ng_ref, win_ref, cw_ref, cb_ref, bdq_ref, bdk_ref, bdv_ref,
                        wg_ref, bg_ref, conv0_ref, *rest, ns, seg):
    n_hand = len(_HANDOFF)
    dst = dict(zip(_HANDOFF_NAMES, rest[:n_hand]))
    conv_ref, xpad_ref = rest[n_hand:]
    pad = SUBLANES
    xpad_ref[:, pad - (CONV_W - 1):pad, :] = conv0_ref[...]
    _interleave(_frontend(x_ref, ng_ref, win_ref, cw_ref, cb_ref, bdq_ref, bdk_ref, bdv_ref,
                          wg_ref, bg_ref, xpad_ref, dst, ns=ns, seg=seg))
    conv_ref[...] = xpad_ref[:, pad + seg - (CONV_W - 1):pad + seg, :]


def _mlstm_back_kernel(*refs, ns, seg):
    n_hand = len(_HANDOFF)
    xres_ref = refs[0]
    src = dict(zip(_HANDOFF_NAMES, refs[1:1 + n_hand]))
    (c0_ref, n0_ref, m0_ref, hng_ref, skip_ref, wout_ref, fg_ref,
     y_ref, c_ref, n_ref, m_ref) = refs[1 + n_hand:]
    _interleave(_backend(xres_ref, src, c0_ref, n0_ref, m0_ref, hng_ref, skip_ref, wout_ref,
                         fg_ref, y_ref, c_ref, n_ref, m_ref, ns=ns, seg=seg))


def _state_specs(ns, index):
    dh = HEAD_DIM
    return [pl.BlockSpec((ns, HEADS, dh, dh), lambda *i: (index(*i), 0, 0, 0)),
            pl.BlockSpec((ns, HEADS, dh), lambda *i: (index(*i), 0, 0)),
            pl.BlockSpec((ns, 1, LANES), lambda *i: (index(*i), 0, 0))]


def _state_shapes(n_streams):
    dh = HEAD_DIM
    return [jax.ShapeDtypeStruct((n_streams, HEADS, dh, dh), F32),
            jax.ShapeDtypeStruct((n_streams, HEADS, dh), F32),
            jax.ShapeDtypeStruct((n_streams, 1, LANES), F32)]


def _front_weight_specs(w):
    nt = D_INNER // MXU_DIM
    e = D_INNER
    return [_const_spec((1, D_MODEL)), _const_spec(w['w_in'].shape),
            _const_spec((CONV_W, e)), _const_spec((1, e)),
            _const_spec((nt, MXU_DIM, MXU_DIM)), _const_spec((nt, MXU_DIM, MXU_DIM)),
            _const_spec((nt, MXU_DIM, MXU_DIM)),
            _const_spec((2, e, 2 * LANES)), _const_spec((1, 2 * LANES))]


def _front_weight_args(w):
    return [w['norm_g'], w['w_in'], w['conv_w'], w['conv_b'], w['bdq'], w['bdk'], w['bdv'],
            w['wg'], w['bg']]


def _mlstm_prompt(x, w, *, n_streams, seg):
    e = D_INNER
    n_tiles = x.shape[0] // (n_streams * seg)
    cur_spec = pl.BlockSpec(
        (seg, D_MODEL), lambda g, t: (g * n_tiles + jnp.minimum(t, n_tiles - 1), 0))
    prev_spec = pl.BlockSpec(
        (seg, D_MODEL), lambda g, t: (g * n_tiles + jnp.maximum(t - 1, 0), 0))
    conv_spec = pl.BlockSpec((1, CONV_W - 1, e), lambda g, t: (g, 0, 0))
    back_specs = [_const_spec((1, e)), _const_spec((1, e)), _const_spec(w['w_out'].shape)]
    fw = _front_weight_specs(w)
    in_specs = [cur_spec, prev_spec, fw[0], _const_spec((1, D_MODEL))] + fw[1:] + back_specs
    fa = _front_weight_args(w)
    args = [x, x, fa[0], w['final_g']] + fa[1:] + [w['hn_g'], w['skip'], w['w_out']]
    handoff = [pltpu.VMEM((seg, width), dtype)
               for _ in range(2) for _, width, dtype in _HANDOFF]
    return pl.pallas_call(
        functools.partial(_mlstm_pipelined_kernel, seg=seg),
        grid=(n_streams, n_tiles + 1),
        in_specs=in_specs,
        out_specs=[prev_spec] + _state_specs(1, lambda g, t: g) + [conv_spec],
        out_shape=([jax.ShapeDtypeStruct(x.shape, F32)] + _state_shapes(n_streams)
                   + [jax.ShapeDtypeStruct((n_streams, CONV_W - 1, e), F32)]),
        scratch_shapes=[pltpu.VMEM((1, SUBLANES + seg, e), F32)] + handoff,
        compiler_params=pltpu.CompilerParams(
            dimension_semantics=("arbitrary", "arbitrary"),
            vmem_limit_bytes=VMEM_LIMIT_BYTES),
        name="mlstm_prompt",
    )(*args)


def _mlstm_sample(x, w, state, *, seg, ns_front, ns_back):
    e = D_INNER
    c0, n0, m0, conv0 = state
    n_streams = c0.shape[0]
    rows_f = ns_front * seg
    hand_shapes = [jax.ShapeDtypeStruct((x.shape[0], width), dtype)
                   for _, width, dtype in _HANDOFF]
    conv_spec = pl.BlockSpec((ns_front, CONV_W - 1, e), lambda i: (i, 0, 0))
    res = pl.pallas_call(
        functools.partial(_mlstm_front_kernel, ns=ns_front, seg=seg),
        grid=(n_streams // ns_front,),
        in_specs=([pl.BlockSpec((rows_f, D_MODEL), lambda i: (i, 0))]
                  + _front_weight_specs(w) + [conv_spec]),
        out_specs=([pl.BlockSpec((rows_f, width), lambda i: (i, 0)) for _, width, _ in _HANDOFF]
                   + [conv_spec]),
        out_shape=hand_shapes + [jax.ShapeDtypeStruct((n_streams, CONV_W - 1, e), F32)],
        scratch_shapes=[pltpu.VMEM((ns_front, SUBLANES + seg, e), F32)],
        compiler_params=pltpu.CompilerParams(
            dimension_semantics=("arbitrary",), vmem_limit_bytes=VMEM_LIMIT_BYTES),
        name="mlstm_sample_front",
    )(x, *_front_weight_args(w), conv0)
    hand, conv_out = res[:-1], res[-1]

    rows_b = ns_back * seg
    row_specs = ([pl.BlockSpec((rows_b, D_MODEL), lambda i: (i, 0))]
                 + [pl.BlockSpec((rows_b, width), lambda i: (i, 0)) for _, width, _ in _HANDOFF])
    state_specs = _state_specs(ns_back, lambda i: i)
    y, c_out, n_out, m_out = pl.pallas_call(
        functools.partial(_mlstm_back_kernel, ns=ns_back, seg=seg),
        grid=(n_streams // ns_back,),
        in_specs=(row_specs + state_specs
                  + [_const_spec((1, e)), _const_spec((1, e)), _const_spec(w['w_out'].shape),
                     _const_spec((1, D_MODEL))]),
        out_specs=[pl.BlockSpec((rows_b, D_MODEL), lambda i: (i, 0))] + state_specs,
        out_shape=[jax.ShapeDtypeStruct(x.shape, F32)] + _state_shapes(n_streams),
        compiler_params=pltpu.CompilerParams(
            dimension_semantics=("arbitrary",), vmem_limit_bytes=VMEM_LIMIT_BYTES),
        name="mlstm_sample_back",
    )(x, *hand, c0, n0, m0, w['hn_g'], w['skip'], w['w_out'], w['final_g'])
    return y, c_out, n_out, m_out, conv_out


def _expand_blockdiag(w):
    nt = D_INNER // MXU_DIM
    rows = w.reshape(3, nt, MXU_DIM, QKV_BLOCK)
    tiled = jnp.tile(rows, (1, 1, 1, MXU_DIM // QKV_BLOCK))
    blk = np.arange(MXU_DIM) // QKV_BLOCK
    return jnp.where(blk[:, None] == blk[None, :], tiled, 0.0).astype(BF16)


def _fold_kernel(bd_ref, wg_ref, o_ref):
    for j in range(bd_ref.shape[1]):
        r0, r1 = j * MXU_DIM, (j + 1) * MXU_DIM
        o_ref[0, r0:r1, :] = (_dot(bd_ref[0, j], wg_ref[0, r0:r1, :])
                              + _dot(bd_ref[1, j], wg_ref[1, r0:r1, :])).astype(BF16)
        o_ref[1, r0:r1, :] = _dot(bd_ref[2, j], wg_ref[2, r0:r1, :]).astype(BF16)


def _fold_gate_weights(bd, wg):
    return pl.pallas_call(
        _fold_kernel,
        out_shape=jax.ShapeDtypeStruct((2, D_INNER, wg.shape[-1]), BF16),
        name="fold_gate_weights",
    )(bd, wg)


def _head_lanes(w):
    lead = w.shape[:-1]
    z = jnp.zeros(lead + (LANES - HEADS,), w.dtype)
    return jnp.concatenate([w[..., :HEADS], z, w[..., HEADS:], z], axis=-1)


def kernel(x_prompt, x_sample, state_mlstm_C, state_mlstm_n, state_mlstm_m, state_mlstm_conv, norm_g, final_norm_g, a_w_in, a_ln_g, a_ln_b, a_w_s, a_b_s, a_w_out, b_w_in, b_conv_w, b_conv_b, b_wq, b_wk, b_wv, b_w_gates, b_b_gates, b_hnorm_g, b_skip, b_w_out):
    bsz, seq, d = x_prompt.shape
    dec_b, dec_t, _ = x_sample.shape
    e = D_INNER
    assert norm_g.shape[0] == 2 and a_w_in.shape[0] == 1 and b_w_in.shape[0] == 1
    assert d == D_MODEL and seq % MXU_DIM == 0 and SGU_CHUNK % dec_t == 0

    xp = x_prompt.reshape(bsz * seq, d)
    xs = x_sample.reshape(dec_b * dec_t, d)

    a_in = _col_slabs(a_w_in[0])
    a_out = _col_slabs(a_w_out[0])
    ng0 = norm_g[0].reshape(1, d)
    lng = a_ln_g[0].reshape(1, e)
    lnb = a_ln_b[0].reshape(1, e)
    (xp1,) = _sgu_layer(xp, ng0, a_in, lng, lnb, a_w_s[0], a_b_s[0].T, a_out,
                        tm=MXU_DIM, want_v=False)
    per_tile = SGU_CHUNK // dec_t
    w_head = a_w_s[0][:, :dec_t, :dec_t]
    ws_s = jnp.einsum('ab,gts->gatbs', np.eye(per_tile, dtype=np.float32), w_head)
    ws_s = ws_s.reshape(SGU_GROUPS, SGU_CHUNK, SGU_CHUNK)
    bs_s = jnp.tile(a_b_s[0][:, :dec_t], (1, per_tile)).T
    xs1, v_s = _sgu_layer(xs, ng0, a_in, lng, lnb, ws_s, bs_s, a_out,
                          tm=SGU_CHUNK, want_v=True)

    bd = _expand_blockdiag(jnp.stack([b_wq[0], b_wk[0], b_wv[0]]))
    wg = _head_lanes(b_w_gates[0]).astype(BF16).reshape(3, e, 2 * LANES)
    w = dict(norm_g=norm_g[1].reshape(1, d), final_g=final_norm_g.reshape(1, d),
             w_in=_col_slabs(b_w_in[0]), conv_w=b_conv_w[0], conv_b=b_conv_b[0].reshape(1, e),
             bdq=bd[0], bdk=bd[1], bdv=bd[2],
             wg=_fold_gate_weights(bd, wg),
             bg=_head_lanes(b_b_gates[0]).reshape(1, 2 * LANES),
             hn_g=b_hnorm_g[0].reshape(1, e), skip=b_skip[0].reshape(1, e),
             w_out=_col_slabs(b_w_out[0]))
    yp, c_p, n_p, m_p, conv_p = _mlstm_prompt(xp1, w, n_streams=bsz, seg=MXU_DIM)
    m0 = jnp.pad(state_mlstm_m[0], ((0, 0), (0, LANES - HEADS))).reshape(dec_b, 1, LANES)
    ys, c_s, n_s, m_s, conv_s = _mlstm_sample(
        xs1, w, (state_mlstm_C[0], state_mlstm_n[0], m0, state_mlstm_conv[0]),
        seg=dec_t, ns_front=dec_b // 2, ns_back=1)

    return (yp.reshape(bsz, seq, d),
            ys.reshape(dec_b, dec_t, d),
            v_s.reshape(1, dec_b, dec_t, e),
            c_p[None], n_p[None], m_p[:, 0, :HEADS][None], conv_p[None],
            c_s[None], n_s[None], m_s[:, 0, :HEADS][None], conv_s[None])
```

```python
import functools
import math

import numpy as np
import jax
import jax.numpy as jnp
from jax import lax
from jax.experimental import pallas as pl
from jax.experimental.pallas import tpu as pltpu

D_MODEL = 1024
D_INNER = 2048
SGU_CHUNK = 128
SGU_GROUPS = 8
SGU_GROUP_DIM = D_INNER // SGU_GROUPS
HEADS = 4
HEAD_DIM = D_INNER // HEADS
QKV_BLOCK = 4
CONV_W = 4
RMS_EPS = 1e-6
LN_EPS = 1e-5

LANES = 128
SUBLANES = 8
MXU_DIM = 256
VMEM_LIMIT_BYTES = 60 * 1024 * 1024

BF16 = jnp.bfloat16
F32 = jnp.float32
LOG2E = math.log2(math.e)

assert SGU_GROUP_DIM == MXU_DIM


def _dot(a, b):
    return jnp.dot(a, b, preferred_element_type=F32)


def _rmsnorm(x, g):
    return x * lax.rsqrt(jnp.mean(x * x, axis=-1, keepdims=True) + RMS_EPS) * g


def _log_sigmoid(x):
    return jnp.minimum(x, 0.0) - jnp.log1p(jnp.exp(-jnp.abs(x)))


def _prefix_rows(x, op, fill):
    n = x.shape[0]
    row = lax.broadcasted_iota(jnp.int32, x.shape, 0)
    shift = 1
    while shift < n:
        if shift % SUBLANES == 0:
            shifted = jnp.concatenate(
                [jnp.full((shift, x.shape[1]), fill, x.dtype), x[:n - shift]], axis=0)
        else:
            shifted = jnp.where(row >= shift, pltpu.roll(x, shift, 0), fill)
        x = op(x, shifted)
        shift *= 2
    return x


def _const_spec(shape):
    zeros = (0,) * len(shape)
    return pl.BlockSpec(shape, lambda *_: zeros, pipeline_mode=pl.Buffered(1))


def _slab_kernel(w_ref, o_ref):
    for j in range(o_ref.shape[0]):
        o_ref[j] = w_ref[:, j * MXU_DIM:(j + 1) * MXU_DIM].astype(BF16)


def _col_slabs(w):
    k, n = w.shape
    rows = MXU_DIM
    return pl.pallas_call(
        _slab_kernel,
        grid=(k // rows,),
        in_specs=[pl.BlockSpec((rows, n), lambda i: (i, 0))],
        out_specs=pl.BlockSpec((n // MXU_DIM, rows, MXU_DIM), lambda i: (0, i, 0)),
        out_shape=jax.ShapeDtypeStruct((n // MXU_DIM, k, MXU_DIM), BF16),
        compiler_params=pltpu.CompilerParams(dimension_semantics=("arbitrary",)),
        name="weight_slabs",
    )(w)


def _sgu_kernel(x_ref, ng_ref, win_ref, lng_ref, lnb_ref, ws_ref, bs_ref, wout_ref,
                y_ref, *v_refs, tm):
    _interleave(_sgu_stages(x_ref, ng_ref, win_ref, lng_ref, lnb_ref, ws_ref, bs_ref, wout_ref,
                            y_ref, v_refs, tm=tm))


def _sgu_stages(x_ref, ng_ref, win_ref, lng_ref, lnb_ref, ws_ref, bs_ref, wout_ref,
                y_ref, v_refs, *, tm):
    ng = SGU_GROUPS
    e = D_INNER
    x = x_ref[...]
    xn = _rmsnorm(x, ng_ref[...]).astype(BF16)

    v_g, uz_g = [], []
    row_sum = jnp.zeros((tm, 1), F32)
    for g in range(ng):
        v = jax.nn.gelu(_dot(xn, win_ref[ng + g]))
        row_sum = row_sum + jnp.sum(v, axis=1, keepdims=True)
        v_g.append(v)
        if g % 4 == 3:
            yield
    mu = row_sum * (1.0 / e)
    sq_sum = jnp.zeros((tm, 1), F32)
    for g in range(ng):
        u = jax.nn.gelu(_dot(xn, win_ref[g]))
        z = _dot(xn, win_ref[2 * ng + g])
        uz_g.append(u * jax.nn.silu(z))
        v_g[g] = v_g[g] - mu
        sq_sum = sq_sum + jnp.sum(v_g[g] * v_g[g], axis=1, keepdims=True)
        if g % 4 == 3:
            yield
    rstd = lax.rsqrt(sq_sum * (1.0 / e) + LN_EPS)

    ti = lax.broadcasted_iota(jnp.int32, (SGU_CHUNK, SGU_CHUNK), 0)
    si = lax.broadcasted_iota(jnp.int32, (SGU_CHUNK, SGU_CHUNK), 1)
    causal = si <= ti
    s_g = []
    for g in range(ng):
        lo, hi = g * SGU_GROUP_DIM, (g + 1) * SGU_GROUP_DIM
        vn = v_g[g] * rstd * lng_ref[:, lo:hi] + lnb_ref[:, lo:hi]
        if v_refs:
            v_refs[0][:, lo:hi] = vn
        vb = vn.astype(BF16)
        wm = jnp.where(causal, ws_ref[g], 0.0).astype(BF16)
        bias = bs_ref[:, g:g + 1]
        s_g.append(jnp.concatenate(
            [_dot(wm, vb[c * SGU_CHUNK:(c + 1) * SGU_CHUNK]) + bias
             for c in range(tm // SGU_CHUNK)], axis=0))
    yield
    n_out = D_MODEL // MXU_DIM
    acc = [None] * n_out
    for g in range(ng):
        lo, hi = g * SGU_GROUP_DIM, (g + 1) * SGU_GROUP_DIM
        out = (uz_g[g] * s_g[g]).astype(BF16)
        for n in range(n_out):
            part = _dot(out, wout_ref[n, lo:hi, :])
            acc[n] = part if acc[n] is None else acc[n] + part
        if g == ng // 2 - 1:
            yield
    y_ref[...] = x + jnp.concatenate(acc, axis=1)


def _sgu_layer(x, norm_g, w_in, ln_g, ln_b, w_s, b_s, w_out, *, tm, want_v):
    rows = x.shape[0]
    e = D_INNER
    row_spec = pl.BlockSpec((tm, D_MODEL), lambda i: (i, 0))
    out_shape = [jax.ShapeDtypeStruct((rows, D_MODEL), F32)]
    out_specs = [row_spec]
    if want_v:
        out_shape.append(jax.ShapeDtypeStruct((rows, e), F32))
        out_specs.append(pl.BlockSpec((tm, e), lambda i: (i, 0)))
    res = pl.pallas_call(
        functools.partial(_sgu_kernel, tm=tm),
        grid=(rows // tm,),
        in_specs=[row_spec,
                  _const_spec((1, D_MODEL)),
                  _const_spec(w_in.shape),
                  _const_spec((1, e)),
                  _const_spec((1, e)),
                  _const_spec((SGU_GROUPS, SGU_CHUNK, SGU_CHUNK)),
                  _const_spec((SGU_CHUNK, SGU_GROUPS)),
                  _const_spec(w_out.shape)],
        out_specs=out_specs,
        out_shape=out_shape,
        compiler_params=pltpu.CompilerParams(
            dimension_semantics=("arbitrary",), vmem_limit_bytes=VMEM_LIMIT_BYTES),
        name="sgu_layer_v" if want_v else "sgu_layer",
    )(x, norm_g, w_in, ln_g, ln_b, w_s, b_s, w_out)
    return res


def _cols_to_rows(a, seg, heads):
    if seg % LANES == 0:
        return a.T[0:SUBLANES, :]
    ti = lax.broadcasted_iota(jnp.int32, (seg, seg), 0)
    si = lax.broadcasted_iota(jnp.int32, (seg, seg), 1)
    return jnp.concatenate(
        [jnp.sum(jnp.where(ti == si, a[:, h:h + 1], 0.0), axis=0, keepdims=True)
         for h in range(heads)], axis=0)


_HANDOFF = (('xc', D_INNER, F32), ('sz', D_INNER, F32), ('qb', D_INNER, BF16),
            ('ks', D_INNER, F32), ('vb', D_INNER, BF16), ('gates', 2 * LANES, F32))
_HANDOFF_NAMES = tuple(name for name, _, _ in _HANDOFF)


def _interleave(*gens):
    live = list(gens)
    while live:
        for gen in list(live):
            try:
                next(gen)
            except StopIteration:
                live.remove(gen)


def _frontend(x_ref, ng_ref, win_ref, cw_ref, cb_ref, bdq_ref, bdk_ref, bdv_ref, wg_ref,
              bg_ref, xpad_ref, dst, *, ns, seg):
    e, dh = D_INNER, HEAD_DIM
    pad = SUBLANES
    rows = ns * seg
    tiles_per_head = dh // MXU_DIM
    xn = _rmsnorm(x_ref[...], ng_ref[...]).astype(BF16)

    def project(h):
        t0 = h * tiles_per_head
        xm_ = jnp.concatenate(
            [_dot(xn, win_ref[t0 + j]) for j in range(tiles_per_head)], axis=1)
        z_ = jnp.concatenate(
            [_dot(xn, win_ref[e // MXU_DIM + t0 + j]) for j in range(tiles_per_head)], axis=1)
        return xm_, z_

    projected = project(0)
    gates = bg_ref[...]
    yield
    for h in range(HEADS):
        c0, c1 = h * dh, (h + 1) * dh
        t0 = h * tiles_per_head
        xm, z = projected
        if h + 1 < HEADS:
            projected = project(h + 1)
        xpad_ref[:, pad:pad + seg, c0:c1] = xm.reshape(ns, seg, dh)
        pre = cb_ref[:, c0:c1] + cw_ref[CONV_W - 1:CONV_W, c0:c1] * xm
        for d in range(1, CONV_W):
            shifted = xpad_ref[:, pad - d:pad - d + seg, c0:c1].reshape(rows, dh)
            pre = pre + cw_ref[CONV_W - 1 - d:CONV_W - d, c0:c1] * shifted
        xc = jax.nn.silu(pre)
        xcb = xc.astype(BF16)
        xmb = xm.astype(BF16)

        def blockdiag(a, w_ref):
            return jnp.concatenate(
                [_dot(a[:, j * MXU_DIM:(j + 1) * MXU_DIM], w_ref[t0 + j])
                 for j in range(tiles_per_head)], axis=1)

        q = blockdiag(xcb, bdq_ref)
        k = blockdiag(xcb, bdk_ref)
        v = blockdiag(xmb, bdv_ref)
        qb, vb = q.astype(BF16), v.astype(BF16)
        gates = gates + _dot(xcb, wg_ref[0, c0:c1, :]) + _dot(xmb, wg_ref[1, c0:c1, :])
        dst['xc'][:, c0:c1] = xc
        dst['sz'][:, c0:c1] = jax.nn.silu(z)
        dst['qb'][:, c0:c1] = qb
        dst['ks'][:, c0:c1] = k * (dh ** -0.5)
        dst['vb'][:, c0:c1] = vb
        if h == HEADS - 1:
            dst['gates'][...] = gates
        yield


def _backend(xres_ref, src, cin_ref, nin_ref, min_ref, hng_ref, skip_ref, wout_ref, fg_ref,
             y_ref, c_ref, n_ref, m_ref, *, ns, seg, heads=HEADS, head0=None, acc_ref=None):
    dh = HEAD_DIM
    n_out = D_MODEL // MXU_DIM
    ti = lax.broadcasted_iota(jnp.int32, (seg, seg), 0)
    si = lax.broadcasted_iota(jnp.int32, (seg, seg), 1)
    causal = si <= ti
    pairs = [(s, h) for s in range(ns) for h in range(heads)]

    def block(name, s, h):
        return src[name][s * seg:(s + 1) * seg, h * dh:(h + 1) * dh]

    def local(x):
        if head0 is None:
            return x
        return pltpu.roll(x, (LANES - head0) % LANES, 1)

    qc = {(s, h): _dot(block('qb', s, h), cin_ref[s, h].astype(BF16)) for s, h in pairs}
    gt = []
    for s in range(ns):
        g = src['gates'][s * seg:(s + 1) * seg, :]
        ig = g[:, 0:LANES]
        lf = _log_sigmoid(g[:, LANES:2 * LANES])
        b = _prefix_rows(lf, jnp.add, 0.0)
        a = ig - b
        cm = _prefix_rows(a, jnp.maximum, -jnp.inf)
        m_prev = min_ref[s]
        neg_mt_plus_b = -jnp.maximum(m_prev, cm)
        b_last = b[seg - 1:seg, :]
        m_new = b_last + jnp.maximum(m_prev, cm[seg - 1:seg, :])
        decay = jnp.broadcast_to(jnp.exp(b_last + m_prev - m_new), (SUBLANES, LANES))
        gt.append(dict(
            w_inter=local(jnp.exp(m_prev + neg_mt_plus_b)),
            inv_floor=local(jnp.exp(neg_mt_plus_b - b)),
            w=local(jnp.exp(b_last + a - m_new)),
            decay=local(decay)[0:1, :],
            a_rows=_cols_to_rows(local(a), seg, heads) * LOG2E,
            u=local(neg_mt_plus_b) * LOG2E,
            n_prev=nin_ref[s]))
        m_ref[s] = m_new
    yield

    sc = {}
    for s, h in pairs:
        w_intra = jnp.where(
            causal, jnp.exp2(gt[s]['u'][:, h:h + 1] + gt[s]['a_rows'][h:h + 1, :]), 0.0)
        sc[s, h] = lax.dot_general(block('qb', s, h), block('ks', s, h).astype(BF16),
                                   (((1,), (1,)), ((), ())),
                                   preferred_element_type=F32) * w_intra
    yield

    num = {(s, h): _dot(sc[s, h].astype(BF16), block('vb', s, h))
           + gt[s]['w_inter'][:, h:h + 1] * qc[s, h] for s, h in pairs}
    yield

    out, n_new = {}, {}
    for s, h in pairs:
        c0, c1 = h * dh, (h + 1) * dh
        n_prev = gt[s]['n_prev'][h:h + 1, :]
        qn = jnp.sum(block('qb', s, h).astype(F32) * n_prev, axis=1, keepdims=True)
        den = jnp.sum(sc[s, h], axis=1, keepdims=True) + gt[s]['w_inter'][:, h:h + 1] * qn
        r = 1.0 / jnp.maximum(jnp.abs(den), gt[s]['inv_floor'][:, h:h + 1])
        cen = num[s, h] - jnp.mean(num[s, h], axis=1, keepdims=True)
        var = jnp.mean(cen * cen, axis=1, keepdims=True)
        hn = cen * (r * lax.rsqrt(r * r * var + LN_EPS)) * hng_ref[:, c0:c1]
        out[s, h] = ((hn + skip_ref[:, c0:c1] * block('xc', s, h))
                     * block('sz', s, h)).astype(BF16)

        decay = gt[s]['decay'][:, h:h + 1]
        kw = block('ks', s, h) * gt[s]['w'][:, h:h + 1]
        upd = lax.dot_general(kw.astype(BF16), block('vb', s, h), (((0,), (0,)), ((), ())),
                              preferred_element_type=F32)
        c_ref[s, h] = decay * cin_ref[s, h] + upd
        n_new[s, h] = decay * n_prev + jnp.sum(kw, axis=0, keepdims=True)
    for s in range(ns):
        n_ref[s] = jnp.concatenate([n_new[s, h] for h in range(heads)], axis=0)
    yield

    y_rows = []
    for s in range(ns):
        acc = [None] * n_out
        for h in range(heads):
            for n in range(n_out):
                part = _dot(out[s, h], wout_ref[n, h * dh:(h + 1) * dh, :])
                acc[n] = part if acc[n] is None else acc[n] + part
        y_rows.append(jnp.concatenate(acc, axis=1))
    proj = y_rows[0] if ns == 1 else jnp.concatenate(y_rows, axis=0)
    if acc_ref is not None:
        proj = jnp.where(head0 == 0, 0.0, acc_ref[...]) + proj
        acc_ref[...] = proj
    y_ref[...] = _rmsnorm(xres_ref[...] + proj, fg_ref[...])


def _mlstm_pipelined_kernel(x_ref, xprev_ref, ng_ref, fg_ref, win_ref, cw_ref, cb_ref,
                            bdq_ref, bdk_ref, bdv_ref, wg_ref, bg_ref, hng_ref, skip_ref,
                            wout_ref, y_ref, c_ref, n_ref, m_ref, conv_ref, xpad_ref,
                            *handoff, seg):
    t = pl.program_id(1)
    e = D_INNER
    pad = SUBLANES
    n_hand = len(_HANDOFF)
    sets = (dict(zip(_HANDOFF_NAMES, handoff[:n_hand])),
            dict(zip(_HANDOFF_NAMES, handoff[n_hand:])))

    @pl.when(t == 0)
    def _():
        xpad_ref[:, 0:pad, :] = jnp.zeros((1, pad, e), F32)
        for ref in sets[1].values():
            ref[...] = jnp.zeros(ref.shape, ref.dtype)

    @pl.when(t <= 1)
    def _():
        c_ref[...] = jnp.zeros(c_ref.shape, F32)
        n_ref[...] = jnp.zeros(n_ref.shape, F32)
        m_ref[...] = jnp.zeros(m_ref.shape, F32)

    def step(dst, src):
        front = _frontend(x_ref, ng_ref, win_ref, cw_ref, cb_ref, bdq_ref, bdk_ref, bdv_ref,
                          wg_ref, bg_ref, xpad_ref, dst, ns=1, seg=seg)
        back = _backend(xprev_ref, src, c_ref, n_ref, m_ref, hng_ref, skip_ref, wout_ref,
                        fg_ref, y_ref, c_ref, n_ref, m_ref, ns=1, seg=seg)
        _interleave(back, front)
        conv_ref[...] = xpad_ref[:, pad + seg - (CONV_W - 1):pad + seg, :]
        xpad_ref[:, 0:pad, :] = xpad_ref[:, seg:seg + pad, :]

    @pl.when(t % 2 == 0)
    def _():
        step(sets[0], sets[1])

    @pl.when(t % 2 == 1)
    def _():
        step(sets[1], sets[0])


def _mlstm_front_kernel(x_ref, ng_ref, win_ref, cw_ref, cb_ref, bdq_ref, bdk_ref, bdv_ref,
                        wg_ref, bg_ref, conv0_ref, *rest, ns, seg):
    n_hand = len(_HANDOFF)
    dst = dict(zip(_HANDOFF_NAMES, rest[:n_hand]))
    conv_ref, xpad_ref = rest[n_hand:]
    pad = SUBLANES
    xpad_ref[:, pad - (CONV_W - 1):pad, :] = conv0_ref[...]
    _interleave(_frontend(x_ref, ng_ref, win_ref, cw_ref, cb_ref, bdq_ref, bdk_ref, bdv_ref,
                          wg_ref, bg_ref, xpad_ref, dst, ns=ns, seg=seg))
    conv_ref[...] = xpad_ref[:, pad + seg - (CONV_W - 1):pad + seg, :]


HEAD_GROUP = 2


def _sgu_recurrence_kernel(*refs, tm, seg):
    n_hand = len(_HANDOFF)
    sgu_refs, rest = refs[:8], refs[8:]
    xres_ref = rest[0]
    src = dict(zip(_HANDOFF_NAMES, rest[1:1 + n_hand]))
    (c0_ref, n0_ref, m0_ref, hng_ref, skip_ref, wout_ref, fg_ref,
     y_ref, ys_ref, c_ref, n_ref, m_ref, acc_ref) = rest[1 + n_hand:]
    i = pl.program_id(0)
    head0 = (i % (HEADS // HEAD_GROUP)) * HEAD_GROUP

    @pl.when(i == 0)
    def _():
        acc_ref[...] = jnp.zeros(acc_ref.shape, F32)

    _interleave(
        _backend(xres_ref, src, c0_ref, n0_ref.at[0], m0_ref, hng_ref, skip_ref, wout_ref,
                 fg_ref, ys_ref, c_ref, n_ref.at[0], m_ref, ns=1, seg=seg,
                 heads=HEAD_GROUP, head0=head0, acc_ref=acc_ref),
        _sgu_stages(*sgu_refs, y_ref, (), tm=tm))


def _state_specs(ns, index):
    dh = HEAD_DIM
    return [pl.BlockSpec((ns, HEADS, dh, dh), lambda *i: (index(*i), 0, 0, 0)),
            pl.BlockSpec((ns, HEADS, dh), lambda *i: (index(*i), 0, 0)),
            pl.BlockSpec((ns, 1, LANES), lambda *i: (index(*i), 0, 0))]


def _state_shapes(n_streams):
    dh = HEAD_DIM
    return [jax.ShapeDtypeStruct((n_streams, HEADS, dh, dh), F32),
            jax.ShapeDtypeStruct((n_streams, HEADS, dh), F32),
            jax.ShapeDtypeStruct((n_streams, 1, LANES), F32)]


def _front_weight_specs(w):
    nt = D_INNER // MXU_DIM
    e = D_INNER
    return [_const_spec((1, D_MODEL)), _const_spec(w['w_in'].shape),
            _const_spec((CONV_W, e)), _const_spec((1, e)),
            _const_spec((nt, MXU_DIM, MXU_DIM)), _const_spec((nt, MXU_DIM, MXU_DIM)),
            _const_spec((nt, MXU_DIM, MXU_DIM)),
            _const_spec((2, e, 2 * LANES)), _const_spec((1, 2 * LANES))]


def _front_weight_args(w):
    return [w['norm_g'], w['w_in'], w['conv_w'], w['conv_b'], w['bdq'], w['bdk'], w['bdv'],
            w['wg'], w['bg']]


def _mlstm_prompt(x, w, *, n_streams, seg):
    e = D_INNER
    n_tiles = x.shape[0] // (n_streams * seg)
    cur_spec = pl.BlockSpec(
        (seg, D_MODEL), lambda g, t: (g * n_tiles + jnp.minimum(t, n_tiles - 1), 0))
    prev_spec = pl.BlockSpec(
        (seg, D_MODEL), lambda g, t: (g * n_tiles + jnp.maximum(t - 1, 0), 0))
    conv_spec = pl.BlockSpec((1, CONV_W - 1, e), lambda g, t: (g, 0, 0))
    back_specs = [_const_spec((1, e)), _const_spec((1, e)), _const_spec(w['w_out'].shape)]
    fw = _front_weight_specs(w)
    in_specs = [cur_spec, prev_spec, fw[0], _const_spec((1, D_MODEL))] + fw[1:] + back_specs
    fa = _front_weight_args(w)
    args = [x, x, fa[0], w['final_g']] + fa[1:] + [w['hn_g'], w['skip'], w['w_out']]
    handoff = [pltpu.VMEM((seg, width), dtype)
               for _ in range(2) for _, width, dtype in _HANDOFF]
    return pl.pallas_call(
        functools.partial(_mlstm_pipelined_kernel, seg=seg),
        grid=(n_streams, n_tiles + 1),
        in_specs=in_specs,
        out_specs=[prev_spec] + _state_specs(1, lambda g, t: g) + [conv_spec],
        out_shape=([jax.ShapeDtypeStruct(x.shape, F32)] + _state_shapes(n_streams)
                   + [jax.ShapeDtypeStruct((n_streams, CONV_W - 1, e), F32)]),
        scratch_shapes=[pltpu.VMEM((1, SUBLANES + seg, e), F32)] + handoff,
        compiler_params=pltpu.CompilerParams(
            dimension_semantics=("arbitrary", "arbitrary"),
            vmem_limit_bytes=VMEM_LIMIT_BYTES),
        name="mlstm_prompt",
    )(*args)


def _mlstm_sample_front(x, w, conv0, *, seg, ns_front):
    e = D_INNER
    n_streams = conv0.shape[0]
    rows_f = ns_front * seg
    hand_shapes = [jax.ShapeDtypeStruct((x.shape[0], width), dtype)
                   for _, width, dtype in _HANDOFF]
    conv_spec = pl.BlockSpec((ns_front, CONV_W - 1, e), lambda i: (i, 0, 0))
    res = pl.pallas_call(
        functools.partial(_mlstm_front_kernel, ns=ns_front, seg=seg),
        grid=(n_streams // ns_front,),
        in_specs=([pl.BlockSpec((rows_f, D_MODEL), lambda i: (i, 0))]
                  + _front_weight_specs(w) + [conv_spec]),
        out_specs=([pl.BlockSpec((rows_f, width), lambda i: (i, 0)) for _, width, _ in _HANDOFF]
                   + [conv_spec]),
        out_shape=hand_shapes + [jax.ShapeDtypeStruct((n_streams, CONV_W - 1, e), F32)],
        scratch_shapes=[pltpu.VMEM((ns_front, SUBLANES + seg, e), F32)],
        compiler_params=pltpu.CompilerParams(
            dimension_semantics=("arbitrary",), vmem_limit_bytes=VMEM_LIMIT_BYTES),
        name="mlstm_sample_front",
    )(x, *_front_weight_args(w), conv0)
    return res[:-1], res[-1]


def _sgu_prompt_with_sample_recurrence(xp, sgu_args, xs, hand, state, w, *, seg):
    e, dh = D_INNER, HEAD_DIM
    c0, n0, m0 = state
    n_streams = c0.shape[0]
    groups = HEADS // HEAD_GROUP
    tm = MXU_DIM
    steps = xp.shape[0] // tm
    assert steps == n_streams * groups, (steps, n_streams, groups)
    gw = HEAD_GROUP * dh
    norm_g, w_in, ln_g, ln_b, w_s, b_s, w_out = sgu_args

    def stream(i):
        return i // groups

    def group(i):
        return i % groups

    sgu_specs = [pl.BlockSpec((tm, D_MODEL), lambda i: (i, 0)),
                 _const_spec((1, D_MODEL)), _const_spec(w_in.shape),
                 _const_spec((1, e)), _const_spec((1, e)),
                 _const_spec((SGU_GROUPS, SGU_CHUNK, SGU_CHUNK)),
                 _const_spec((SGU_CHUNK, SGU_GROUPS)), _const_spec(w_out.shape)]
    hand_specs = [pl.BlockSpec((seg, gw), lambda i: (stream(i), group(i))) if width == e
                  else pl.BlockSpec((seg, width), lambda i: (stream(i), 0))
                  for _, width, _ in _HANDOFF]
    c_spec = pl.BlockSpec((1, HEAD_GROUP, dh, dh), lambda i: (stream(i), group(i), 0, 0))
    n_spec = pl.BlockSpec((1, 1, HEAD_GROUP, dh), lambda i: (stream(i), group(i), 0, 0))
    m_spec = pl.BlockSpec((1, 1, LANES), lambda i: (stream(i), 0, 0))
    ys_spec = pl.BlockSpec((seg, D_MODEL), lambda i: (stream(i), 0))
    head_vec_spec = pl.BlockSpec((1, gw), lambda i: (0, group(i)))
    wout_spec = pl.BlockSpec((D_MODEL // MXU_DIM, gw, MXU_DIM), lambda i: (0, group(i), 0))
    n0g = n0.reshape(n_streams, groups, HEAD_GROUP, dh)
    xp1, ys, c_out, n_out, m_out = pl.pallas_call(
        functools.partial(_sgu_recurrence_kernel, tm=tm, seg=seg),
        grid=(steps,),
        in_specs=(sgu_specs + [ys_spec] + hand_specs + [c_spec, n_spec, m_spec]
                  + [head_vec_spec, head_vec_spec, wout_spec, _const_spec((1, D_MODEL))]),
        out_specs=[pl.BlockSpec((tm, D_MODEL), lambda i: (i, 0)), ys_spec,
                   c_spec, n_spec, m_spec],
        out_shape=[jax.ShapeDtypeStruct(xp.shape, F32), jax.ShapeDtypeStruct(xs.shape, F32),
                   jax.ShapeDtypeStruct(c0.shape, F32), jax.ShapeDtypeStruct(n0g.shape, F32),
                   jax.ShapeDtypeStruct(m0.shape, F32)],
        scratch_shapes=[pltpu.VMEM((seg, D_MODEL), F32)],
        compiler_params=pltpu.CompilerParams(
            dimension_semantics=("arbitrary",), vmem_limit_bytes=VMEM_LIMIT_BYTES),
        name="sgu_prompt_sample_recurrence",
    )(xp, norm_g, w_in, ln_g, ln_b, w_s, b_s, w_out, xs, *hand, c0, n0g, m0,
      w['hn_g'], w['skip'], w['w_out'], w['final_g'])
    return xp1, ys, c_out, n_out.reshape(n0.shape), m_out


def _expand_blockdiag(w):
    nt = D_INNER // MXU_DIM
    rows = w.reshape(3, nt, MXU_DIM, QKV_BLOCK)
    tiled = jnp.tile(rows, (1, 1, 1, MXU_DIM // QKV_BLOCK))
    blk = np.arange(MXU_DIM) // QKV_BLOCK
    return jnp.where(blk[:, None] == blk[None, :], tiled, 0.0).astype(BF16)


def _fold_kernel(bd_ref, wg_ref, o_ref):
    for j in range(bd_ref.shape[1]):
        r0, r1 = j * MXU_DIM, (j + 1) * MXU_DIM
        o_ref[0, r0:r1, :] = (_dot(bd_ref[0, j], wg_ref[0, r0:r1, :])
                              + _dot(bd_ref[1, j], wg_ref[1, r0:r1, :])).astype(BF16)
        o_ref[1, r0:r1, :] = _dot(bd_ref[2, j], wg_ref[2, r0:r1, :]).astype(BF16)


def _fold_gate_weights(bd, wg):
    return pl.pallas_call(
        _fold_kernel,
        out_shape=jax.ShapeDtypeStruct((2, D_INNER, wg.shape[-1]), BF16),
        name="fold_gate_weights",
    )(bd, wg)


def _head_lanes(w):
    lead = w.shape[:-1]
    z = jnp.zeros(lead + (LANES - HEADS,), w.dtype)
    return jnp.concatenate([w[..., :HEADS], z, w[..., HEADS:], z], axis=-1)


def kernel(x_prompt, x_sample, state_mlstm_C, state_mlstm_n, state_mlstm_m, state_mlstm_conv, norm_g, final_norm_g, a_w_in, a_ln_g, a_ln_b, a_w_s, a_b_s, a_w_out, b_w_in, b_conv_w, b_conv_b, b_wq, b_wk, b_wv, b_w_gates, b_b_gates, b_hnorm_g, b_skip, b_w_out):
    bsz, seq, d = x_prompt.shape
    dec_b, dec_t, _ = x_sample.shape
    e = D_INNER
    assert norm_g.shape[0] == 2 and a_w_in.shape[0] == 1 and b_w_in.shape[0] == 1
    assert d == D_MODEL and seq % MXU_DIM == 0 and SGU_CHUNK % dec_t == 0

    xp = x_prompt.reshape(bsz * seq, d)
    xs = x_sample.reshape(dec_b * dec_t, d)

    a_in = _col_slabs(a_w_in[0])
    a_out = _col_slabs(a_w_out[0])
    ng0 = norm_g[0].reshape(1, d)
    lng = a_ln_g[0].reshape(1, e)
    lnb = a_ln_b[0].reshape(1, e)
    per_tile = SGU_CHUNK // dec_t
    w_head = a_w_s[0][:, :dec_t, :dec_t]
    ws_s = jnp.einsum('ab,gts->gatbs', np.eye(per_tile, dtype=np.float32), w_head)
    ws_s = ws_s.reshape(SGU_GROUPS, SGU_CHUNK, SGU_CHUNK)
    bs_s = jnp.tile(a_b_s[0][:, :dec_t], (1, per_tile)).T
    xs1, v_s = _sgu_layer(xs, ng0, a_in, lng, lnb, ws_s, bs_s, a_out,
                          tm=SGU_CHUNK, want_v=True)

    bd = _expand_blockdiag(jnp.stack([b_wq[0], b_wk[0], b_wv[0]]))
    wg = _head_lanes(b_w_gates[0]).astype(BF16).reshape(3, e, 2 * LANES)
    w = dict(norm_g=norm_g[1].reshape(1, d), final_g=final_norm_g.reshape(1, d),
             w_in=_col_slabs(b_w_in[0]), conv_w=b_conv_w[0], conv_b=b_conv_b[0].reshape(1, e),
             bdq=bd[0], bdk=bd[1], bdv=bd[2],
             wg=_fold_gate_weights(bd, wg),
             bg=_head_lanes(b_b_gates[0]).reshape(1, 2 * LANES),
             hn_g=b_hnorm_g[0].reshape(1, e), skip=b_skip[0].reshape(1, e),
             w_out=_col_slabs(b_w_out[0]))
    hand, conv_s = _mlstm_sample_front(xs1, w, state_mlstm_conv[0], seg=dec_t,
                                       ns_front=dec_b // 2)
    m0 = jnp.pad(state_mlstm_m[0], ((0, 0), (0, LANES - HEADS))).reshape(dec_b, 1, LANES)
    xp1, ys, c_s, n_s, m_s = _sgu_prompt_with_sample_recurrence(
        xp, (ng0, a_in, lng, lnb, a_w_s[0], a_b_s[0].T, a_out), xs1, hand,
        (state_mlstm_C[0], state_mlstm_n[0], m0), w, seg=dec_t)
    yp, c_p, n_p, m_p, conv_p = _mlstm_prompt(xp1, w, n_streams=bsz, seg=MXU_DIM)

    return (yp.reshape(bsz, seq, d),
            ys.reshape(dec_b, dec_t, d),
            v_s.reshape(1, dec_b, dec_t, e),
            c_p[None], n_p[None], m_p[:, 0, :HEADS][None], conv_p[None],
            c_s[None], n_s[None], m_s[:, 0, :HEADS][None], conv_s[None])
```

```python
import functools
import math

import numpy as np
import jax
import jax.numpy as jnp
from jax import lax
from jax.experimental import pallas as pl
from jax.experimental.pallas import tpu as pltpu

D_MODEL = 1024
D_INNER = 2048
SGU_CHUNK = 128
SGU_GROUPS = 8
SGU_GROUP_DIM = D_INNER // SGU_GROUPS
HEADS = 4
HEAD_DIM = D_INNER // HEADS
QKV_BLOCK = 4
CONV_W = 4
RMS_EPS = 1e-6
LN_EPS = 1e-5

LANES = 128
SUBLANES = 8
MXU_DIM = 256
VMEM_LIMIT_BYTES = 60 * 1024 * 1024

BF16 = jnp.bfloat16
F32 = jnp.float32
LOG2E = math.log2(math.e)

assert SGU_GROUP_DIM == MXU_DIM


def _dot(a, b):
    return jnp.dot(a, b, preferred_element_type=F32)


def _rmsnorm(x, g):
    return x * lax.rsqrt(jnp.mean(x * x, axis=-1, keepdims=True) + RMS_EPS) * g


def _log_sigmoid(x):
    return jnp.minimum(x, 0.0) - jnp.log1p(jnp.exp(-jnp.abs(x)))


def _prefix_rows(x, op, fill):
    n = x.shape[0]
    row = lax.broadcasted_iota(jnp.int32, x.shape, 0)
    shift = 1
    while shift < n:
        if shift % SUBLANES == 0:
            shifted = jnp.concatenate(
                [jnp.full((shift, x.shape[1]), fill, x.dtype), x[:n - shift]], axis=0)
        else:
            shifted = jnp.where(row >= shift, pltpu.roll(x, shift, 0), fill)
        x = op(x, shifted)
        shift *= 2
    return x


def _const_spec(shape):
    zeros = (0,) * len(shape)
    return pl.BlockSpec(shape, lambda *_: zeros, pipeline_mode=pl.Buffered(1))


def _slab_kernel(w_ref, o_ref):
    for j in range(o_ref.shape[0]):
        o_ref[j] = w_ref[:, j * MXU_DIM:(j + 1) * MXU_DIM].astype(BF16)


def _col_slabs(w):
    k, n = w.shape
    rows = MXU_DIM
    return pl.pallas_call(
        _slab_kernel,
        grid=(k // rows,),
        in_specs=[pl.BlockSpec((rows, n), lambda i: (i, 0))],
        out_specs=pl.BlockSpec((n // MXU_DIM, rows, MXU_DIM), lambda i: (0, i, 0)),
        out_shape=jax.ShapeDtypeStruct((n // MXU_DIM, k, MXU_DIM), BF16),
        compiler_params=pltpu.CompilerParams(dimension_semantics=("arbitrary",)),
        name="weight_slabs",
    )(w)


def _sgu_kernel(x_ref, ng_ref, win_ref, lng_ref, lnb_ref, ws_ref, bs_ref, wout_ref,
                y_ref, *v_refs, tm):
    _interleave(_sgu_stages(x_ref, ng_ref, win_ref, lng_ref, lnb_ref, ws_ref, bs_ref, wout_ref,
                            y_ref, v_refs, tm=tm))


def _sgu_stages(x_ref, ng_ref, win_ref, lng_ref, lnb_ref, ws_ref, bs_ref, wout_ref,
                y_ref, v_refs, *, tm):
    ng = SGU_GROUPS
    e = D_INNER
    x = x_ref[...]
    xn = _rmsnorm(x, ng_ref[...]).astype(BF16)

    v_g, uz_g = [], []
    row_sum = jnp.zeros((tm, 1), F32)
    for g in range(ng):
        v = jax.nn.gelu(_dot(xn, win_ref[ng + g]))
        row_sum = row_sum + jnp.sum(v, axis=1, keepdims=True)
        v_g.append(v)
        if g % 4 == 3:
            yield
    mu = row_sum * (1.0 / e)
    sq_sum = jnp.zeros((tm, 1), F32)
    for g in range(ng):
        u = jax.nn.gelu(_dot(xn, win_ref[g]))
        z = _dot(xn, win_ref[2 * ng + g])
        uz_g.append(u * jax.nn.silu(z))
        v_g[g] = v_g[g] - mu
        sq_sum = sq_sum + jnp.sum(v_g[g] * v_g[g], axis=1, keepdims=True)
        if g % 4 == 3:
            yield
    rstd = lax.rsqrt(sq_sum * (1.0 / e) + LN_EPS)

    ti = lax.broadcasted_iota(jnp.int32, (SGU_CHUNK, SGU_CHUNK), 0)
    si = lax.broadcasted_iota(jnp.int32, (SGU_CHUNK, SGU_CHUNK), 1)
    causal = si <= ti
    s_g = []
    for g in range(ng):
        lo, hi = g * SGU_GROUP_DIM, (g + 1) * SGU_GROUP_DIM
        vn = v_g[g] * rstd * lng_ref[:, lo:hi] + lnb_ref[:, lo:hi]
        if v_refs:
            v_refs[0][:, lo:hi] = vn
        vb = vn.astype(BF16)
        wm = jnp.where(causal, ws_ref[g], 0.0).astype(BF16)
        bias = bs_ref[:, g:g + 1]
        s_g.append(jnp.concatenate(
            [_dot(wm, vb[c * SGU_CHUNK:(c + 1) * SGU_CHUNK]) + bias
             for c in range(tm // SGU_CHUNK)], axis=0))
    yield
    n_out = D_MODEL // MXU_DIM
    acc = [None] * n_out
    for g in range(ng):
        lo, hi = g * SGU_GROUP_DIM, (g + 1) * SGU_GROUP_DIM
        out = (uz_g[g] * s_g[g]).astype(BF16)
        for n in range(n_out):
            part = _dot(out, wout_ref[n, lo:hi, :])
            acc[n] = part if acc[n] is None else acc[n] + part
        if g == ng // 2 - 1:
            yield
    y_ref[...] = x + jnp.concatenate(acc, axis=1)


def _sgu_layer(x, norm_g, w_in, ln_g, ln_b, w_s, b_s, w_out, *, tm, want_v):
    rows = x.shape[0]
    e = D_INNER
    row_spec = pl.BlockSpec((tm, D_MODEL), lambda i: (i, 0))
    out_shape = [jax.ShapeDtypeStruct((rows, D_MODEL), F32)]
    out_specs = [row_spec]
    if want_v:
        out_shape.append(jax.ShapeDtypeStruct((rows, e), F32))
        out_specs.append(pl.BlockSpec((tm, e), lambda i: (i, 0)))
    res = pl.pallas_call(
        functools.partial(_sgu_kernel, tm=tm),
        grid=(rows // tm,),
        in_specs=[row_spec,
                  _const_spec((1, D_MODEL)),
                  _const_spec(w_in.shape),
                  _const_spec((1, e)),
                  _const_spec((1, e)),
                  _const_spec((SGU_GROUPS, SGU_CHUNK, SGU_CHUNK)),
                  _const_spec((SGU_CHUNK, SGU_GROUPS)),
                  _const_spec(w_out.shape)],
        out_specs=out_specs,
        out_shape=out_shape,
        compiler_params=pltpu.CompilerParams(
            dimension_semantics=("arbitrary",), vmem_limit_bytes=VMEM_LIMIT_BYTES),
        name="sgu_layer_v" if want_v else "sgu_layer",
    )(x, norm_g, w_in, ln_g, ln_b, w_s, b_s, w_out)
    return res


def _cols_to_rows(a, seg, heads):
    if seg % LANES == 0:
        return a.T[0:SUBLANES, :]
    ti = lax.broadcasted_iota(jnp.int32, (seg, seg), 0)
    si = lax.broadcasted_iota(jnp.int32, (seg, seg), 1)
    return jnp.concatenate(
        [jnp.sum(jnp.where(ti == si, a[:, h:h + 1], 0.0), axis=0, keepdims=True)
         for h in range(heads)], axis=0)


_HANDOFF = (('xc', D_INNER, F32), ('sz', D_INNER, F32), ('qb', D_INNER, BF16),
            ('ks', D_INNER, F32), ('vb', D_INNER, BF16), ('gates', 2 * LANES, F32))
_HANDOFF_NAMES = tuple(name for name, _, _ in _HANDOFF)


def _interleave(*gens):
    live = list(gens)
    while live:
        for gen in list(live):
            try:
                next(gen)
            except StopIteration:
                live.remove(gen)


def _frontend(x_ref, ng_ref, win_ref, cw_ref, cb_ref, bdq_ref, bdk_ref, bdv_ref, wg_ref,
              bg_ref, xpad_ref, dst, *, ns, seg):
    e, dh = D_INNER, HEAD_DIM
    pad = SUBLANES
    rows = ns * seg
    tiles_per_head = dh // MXU_DIM
    xn = _rmsnorm(x_ref[...], ng_ref[...]).astype(BF16)

    def project(h):
        t0 = h * tiles_per_head
        xm_ = jnp.concatenate(
            [_dot(xn, win_ref[t0 + j]) for j in range(tiles_per_head)], axis=1)
        z_ = jnp.concatenate(
            [_dot(xn, win_ref[e // MXU_DIM + t0 + j]) for j in range(tiles_per_head)], axis=1)
        return xm_, z_

    projected = project(0)
    gates = bg_ref[...]
    yield
    for h in range(HEADS):
        c0, c1 = h * dh, (h + 1) * dh
        t0 = h * tiles_per_head
        xm, z = projected
        if h + 1 < HEADS:
            projected = project(h + 1)
        xpad_ref[:, pad:pad + seg, c0:c1] = xm.reshape(ns, seg, dh)
        pre = cb_ref[:, c0:c1] + cw_ref[CONV_W - 1:CONV_W, c0:c1] * xm
        for d in range(1, CONV_W):
            shifted = xpad_ref[:, pad - d:pad - d + seg, c0:c1].reshape(rows, dh)
            pre = pre + cw_ref[CONV_W - 1 - d:CONV_W - d, c0:c1] * shifted
        xc = jax.nn.silu(pre)
        xcb = xc.astype(BF16)
        xmb = xm.astype(BF16)

        def blockdiag(a, w_ref):
            return jnp.concatenate(
                [_dot(a[:, j * MXU_DIM:(j + 1) * MXU_DIM], w_ref[t0 + j])
                 for j in range(tiles_per_head)], axis=1)

        q = blockdiag(xcb, bdq_ref)
        k = blockdiag(xcb, bdk_ref)
        v = blockdiag(xmb, bdv_ref)
        qb, vb = q.astype(BF16), v.astype(BF16)
        gates = gates + _dot(xcb, wg_ref[0, c0:c1, :]) + _dot(xmb, wg_ref[1, c0:c1, :])
        dst['xc'][:, c0:c1] = xc
        dst['sz'][:, c0:c1] = jax.nn.silu(z)
        dst['qb'][:, c0:c1] = qb
        dst['ks'][:, c0:c1] = k * (dh ** -0.5)
        dst['vb'][:, c0:c1] = vb
        if h == HEADS - 1:
            dst['gates'][...] = gates
        yield


def _backend(xres_ref, src, cin_ref, nin_ref, min_ref, hng_ref, skip_ref, wout_ref, fg_ref,
             y_ref, c_ref, n_ref, m_ref, *, ns, seg, heads=HEADS, head0=None, acc_ref=None,
             memory_read_first=False):
    dh = HEAD_DIM
    n_out = D_MODEL // MXU_DIM
    ti = lax.broadcasted_iota(jnp.int32, (seg, seg), 0)
    si = lax.broadcasted_iota(jnp.int32, (seg, seg), 1)
    causal = si <= ti
    pairs = [(s, h) for s in range(ns) for h in range(heads)]

    def block(name, s, h):
        return src[name][s * seg:(s + 1) * seg, h * dh:(h + 1) * dh]

    def local(x):
        if head0 is None:
            return x
        return pltpu.roll(x, (LANES - head0) % LANES, 1)

    qc, cprev = {}, {}
    if memory_read_first:
        qc = {(s, h): _dot(block('qb', s, h), cin_ref[s, h].astype(BF16)) for s, h in pairs}
    gt = []
    for s in range(ns):
        g = src['gates'][s * seg:(s + 1) * seg, :]
        ig = g[:, 0:LANES]
        lf = _log_sigmoid(g[:, LANES:2 * LANES])
        b = _prefix_rows(lf, jnp.add, 0.0)
        a = ig - b
        cm = _prefix_rows(a, jnp.maximum, -jnp.inf)
        m_prev = min_ref[s]
        neg_mt_plus_b = -jnp.maximum(m_prev, cm)
        b_last = b[seg - 1:seg, :]
        m_new = b_last + jnp.maximum(m_prev, cm[seg - 1:seg, :])
        decay = jnp.broadcast_to(jnp.exp(b_last + m_prev - m_new), (SUBLANES, LANES))
        gt.append(dict(
            w_inter=local(jnp.exp(m_prev + neg_mt_plus_b)),
            inv_floor=local(jnp.exp(neg_mt_plus_b - b)),
            w=local(jnp.exp(b_last + a - m_new)),
            decay=local(decay)[0:1, :],
            a_rows=_cols_to_rows(local(a), seg, heads) * LOG2E,
            u=local(neg_mt_plus_b) * LOG2E,
            n_prev=nin_ref[s]))
        m_ref[s] = m_new
    yield

    sc = {}
    for s, h in pairs:
        w_intra = jnp.where(
            causal, jnp.exp2(gt[s]['u'][:, h:h + 1] + gt[s]['a_rows'][h:h + 1, :]), 0.0)
        sc[s, h] = lax.dot_general(block('qb', s, h), block('ks', s, h).astype(BF16),
                                   (((1,), (1,)), ((), ())),
                                   preferred_element_type=F32) * w_intra
        if not memory_read_first:
            cprev[s, h] = cin_ref[s, h]
            qc[s, h] = _dot(block('qb', s, h), cprev[s, h].astype(BF16))
    yield

    num = {(s, h): _dot(sc[s, h].astype(BF16), block('vb', s, h))
           + gt[s]['w_inter'][:, h:h + 1] * qc[s, h] for s, h in pairs}
    yield

    out, n_new = {}, {}
    for s, h in pairs:
        c0, c1 = h * dh, (h + 1) * dh
        n_prev = gt[s]['n_prev'][h:h + 1, :]
        qn = jnp.sum(block('qb', s, h).astype(F32) * n_prev, axis=1, keepdims=True)
        den = jnp.sum(sc[s, h], axis=1, keepdims=True) + gt[s]['w_inter'][:, h:h + 1] * qn
        r = 1.0 / jnp.maximum(jnp.abs(den), gt[s]['inv_floor'][:, h:h + 1])
        cen = num[s, h] - jnp.mean(num[s, h], axis=1, keepdims=True)
        var = jnp.mean(cen * cen, axis=1, keepdims=True)
        hn = cen * (r * lax.rsqrt(r * r * var + LN_EPS)) * hng_ref[:, c0:c1]
        out[s, h] = ((hn + skip_ref[:, c0:c1] * block('xc', s, h))
                     * block('sz', s, h)).astype(BF16)

        decay = gt[s]['decay'][:, h:h + 1]
        kw = block('ks', s, h) * gt[s]['w'][:, h:h + 1]
        upd = lax.dot_general(kw.astype(BF16), block('vb', s, h), (((0,), (0,)), ((), ())),
                              preferred_element_type=F32)
        c_old = cin_ref[s, h] if memory_read_first else cprev[s, h]
        c_ref[s, h] = decay * c_old + upd
        n_new[s, h] = decay * n_prev + jnp.sum(kw, axis=0, keepdims=True)
    for s in range(ns):
        n_ref[s] = jnp.concatenate([n_new[s, h] for h in range(heads)], axis=0)
    yield

    y_rows = []
    for s in range(ns):
        acc = [None] * n_out
        for h in range(heads):
            for n in range(n_out):
                part = _dot(out[s, h], wout_ref[n, h * dh:(h + 1) * dh, :])
                acc[n] = part if acc[n] is None else acc[n] + part
        y_rows.append(jnp.concatenate(acc, axis=1))
    proj = y_rows[0] if ns == 1 else jnp.concatenate(y_rows, axis=0)
    if acc_ref is not None:
        proj = jnp.where(head0 == 0, 0.0, acc_ref[...]) + proj
        acc_ref[...] = proj
    y_ref[...] = _rmsnorm(xres_ref[...] + proj, fg_ref[...])


def _mlstm_pipelined_kernel(x_ref, xprev_ref, ng_ref, fg_ref, win_ref, cw_ref, cb_ref,
                            bdq_ref, bdk_ref, bdv_ref, wg_ref, bg_ref, hng_ref, skip_ref,
                            wout_ref, y_ref, c_ref, n_ref, m_ref, conv_ref, xpad_ref,
                            *handoff, seg, n_tiles):
    t = pl.program_id(0)
    e = D_INNER
    pad = SUBLANES
    n_hand = len(_HANDOFF)
    sets = (dict(zip(_HANDOFF_NAMES, handoff[:n_hand])),
            dict(zip(_HANDOFF_NAMES, handoff[n_hand:])))

    @pl.when(t == 0)
    def _():
        for ref in sets[1].values():
            ref[...] = jnp.zeros(ref.shape, ref.dtype)

    @pl.when(t % n_tiles == 0)
    def _():
        xpad_ref[:, 0:pad, :] = jnp.zeros((1, pad, e), F32)

    @pl.when((t == 0) | ((t - 1) % n_tiles == 0))
    def _():
        c_ref[...] = jnp.zeros(c_ref.shape, F32)
        n_ref[...] = jnp.zeros(n_ref.shape, F32)
        m_ref[...] = jnp.zeros(m_ref.shape, F32)

    def step(dst, src):
        front = _frontend(x_ref, ng_ref, win_ref, cw_ref, cb_ref, bdq_ref, bdk_ref, bdv_ref,
                          wg_ref, bg_ref, xpad_ref, dst, ns=1, seg=seg)
        back = _backend(xprev_ref, src, c_ref, n_ref, m_ref, hng_ref, skip_ref, wout_ref,
                        fg_ref, y_ref, c_ref, n_ref, m_ref, ns=1, seg=seg)
        _interleave(back, front)
        conv_ref[...] = xpad_ref[:, pad + seg - (CONV_W - 1):pad + seg, :]
        xpad_ref[:, 0:pad, :] = xpad_ref[:, seg:seg + pad, :]

    @pl.when(t % 2 == 0)
    def _():
        step(sets[0], sets[1])

    @pl.when(t % 2 == 1)
    def _():
        step(sets[1], sets[0])


def _mlstm_front_kernel(x_ref, ng_ref, win_ref, cw_ref, cb_ref, bdq_ref, bdk_ref, bdv_ref,
                        wg_ref, bg_ref, conv0_ref, *rest, ns, seg):
    n_hand = len(_HANDOFF)
    dst = dict(zip(_HANDOFF_NAMES, rest[:n_hand]))
    conv_ref, xpad_ref = rest[n_hand:]
    pad = SUBLANES
    xpad_ref[:, pad - (CONV_W - 1):pad, :] = conv0_ref[...]
    _interleave(_frontend(x_ref, ng_ref, win_ref, cw_ref, cb_ref, bdq_ref, bdk_ref, bdv_ref,
                          wg_ref, bg_ref, xpad_ref, dst, ns=ns, seg=seg))
    conv_ref[...] = xpad_ref[:, pad + seg - (CONV_W - 1):pad + seg, :]


HEAD_GROUP = 2


def _sgu_recurrence_kernel(*refs, tm, seg):
    n_hand = len(_HANDOFF)
    sgu_refs, rest = refs[:8], refs[8:]
    xres_ref = rest[0]
    src = dict(zip(_HANDOFF_NAMES, rest[1:1 + n_hand]))
    (c0_ref, n0_ref, m0_ref, hng_ref, skip_ref, wout_ref, fg_ref,
     y_ref, ys_ref, c_ref, n_ref, m_ref, acc_ref) = rest[1 + n_hand:]
    i = pl.program_id(0)
    head0 = (i % (HEADS // HEAD_GROUP)) * HEAD_GROUP

    @pl.when(i == 0)
    def _():
        acc_ref[...] = jnp.zeros(acc_ref.shape, F32)

    _interleave(
        _backend(xres_ref, src, c0_ref, n0_ref.at[0], m0_ref, hng_ref, skip_ref, wout_ref,
                 fg_ref, ys_ref, c_ref, n_ref.at[0], m_ref, ns=1, seg=seg,
                 heads=HEAD_GROUP, head0=head0, acc_ref=acc_ref, memory_read_first=True),
        _sgu_stages(*sgu_refs, y_ref, (), tm=tm))


def _state_specs(ns, index):
    dh = HEAD_DIM
    return [pl.BlockSpec((ns, HEADS, dh, dh), lambda *i: (index(*i), 0, 0, 0)),
            pl.BlockSpec((ns, HEADS, dh), lambda *i: (index(*i), 0, 0)),
            pl.BlockSpec((ns, 1, LANES), lambda *i: (index(*i), 0, 0))]


def _state_shapes(n_streams):
    dh = HEAD_DIM
    return [jax.ShapeDtypeStruct((n_streams, HEADS, dh, dh), F32),
            jax.ShapeDtypeStruct((n_streams, HEADS, dh), F32),
            jax.ShapeDtypeStruct((n_streams, 1, LANES), F32)]


def _front_weight_specs(w):
    nt = D_INNER // MXU_DIM
    e = D_INNER
    return [_const_spec((1, D_MODEL)), _const_spec(w['w_in'].shape),
            _const_spec((CONV_W, e)), _const_spec((1, e)),
            _const_spec((nt, MXU_DIM, MXU_DIM)), _const_spec((nt, MXU_DIM, MXU_DIM)),
            _const_spec((nt, MXU_DIM, MXU_DIM)),
            _const_spec((2, e, 2 * LANES)), _const_spec((1, 2 * LANES))]


def _front_weight_args(w):
    return [w['norm_g'], w['w_in'], w['conv_w'], w['conv_b'], w['bdq'], w['bdk'], w['bdv'],
            w['wg'], w['bg']]


def _mlstm_prompt(x, w, *, n_streams, seg):
    e = D_INNER
    n_tiles = x.shape[0] // (n_streams * seg)
    total = n_streams * n_tiles

    def front_tile(t):
        return jnp.minimum(t, total - 1)

    def back_tile(t):
        return jnp.maximum(t - 1, 0)

    cur_spec = pl.BlockSpec((seg, D_MODEL), lambda t: (front_tile(t), 0))
    prev_spec = pl.BlockSpec((seg, D_MODEL), lambda t: (back_tile(t), 0))
    conv_spec = pl.BlockSpec((1, CONV_W - 1, e), lambda t: (front_tile(t) // n_tiles, 0, 0))
    back_specs = [_const_spec((1, e)), _const_spec((1, e)), _const_spec(w['w_out'].shape)]
    fw = _front_weight_specs(w)
    in_specs = [cur_spec, prev_spec, fw[0], _const_spec((1, D_MODEL))] + fw[1:] + back_specs
    fa = _front_weight_args(w)
    args = [x, x, fa[0], w['final_g']] + fa[1:] + [w['hn_g'], w['skip'], w['w_out']]
    handoff = [pltpu.VMEM((seg, width), dtype)
               for _ in range(2) for _, width, dtype in _HANDOFF]
    return pl.pallas_call(
        functools.partial(_mlstm_pipelined_kernel, seg=seg, n_tiles=n_tiles),
        grid=(total + 1,),
        in_specs=in_specs,
        out_specs=([prev_spec] + _state_specs(1, lambda t: back_tile(t) // n_tiles)
                   + [conv_spec]),
        out_shape=([jax.ShapeDtypeStruct(x.shape, F32)] + _state_shapes(n_streams)
                   + [jax.ShapeDtypeStruct((n_streams, CONV_W - 1, e), F32)]),
        scratch_shapes=[pltpu.VMEM((1, SUBLANES + seg, e), F32)] + handoff,
        compiler_params=pltpu.CompilerParams(
            dimension_semantics=("arbitrary",),
            vmem_limit_bytes=VMEM_LIMIT_BYTES),
        name="mlstm_prompt",
    )(*args)


def _mlstm_sample_front(x, w, conv0, *, seg, ns_front):
    e = D_INNER
    n_streams = conv0.shape[0]
    rows_f = ns_front * seg
    hand_shapes = [jax.ShapeDtypeStruct((x.shape[0], width), dtype)
                   for _, width, dtype in _HANDOFF]
    conv_spec = pl.BlockSpec((ns_front, CONV_W - 1, e), lambda i: (i, 0, 0))
    res = pl.pallas_call(
        functools.partial(_mlstm_front_kernel, ns=ns_front, seg=seg),
        grid=(n_streams // ns_front,),
        in_specs=([pl.BlockSpec((rows_f, D_MODEL), lambda i: (i, 0))]
                  + _front_weight_specs(w) + [conv_spec]),
        out_specs=([pl.BlockSpec((rows_f, width), lambda i: (i, 0)) for _, width, _ in _HANDOFF]
                   + [conv_spec]),
        out_shape=hand_shapes + [jax.ShapeDtypeStruct((n_streams, CONV_W - 1, e), F32)],
        scratch_shapes=[pltpu.VMEM((ns_front, SUBLANES + seg, e), F32)],
        compiler_params=pltpu.CompilerParams(
            dimension_semantics=("arbitrary",), vmem_limit_bytes=VMEM_LIMIT_BYTES),
        name="mlstm_sample_front",
    )(x, *_front_weight_args(w), conv0)
    return res[:-1], res[-1]


def _sgu_prompt_with_sample_recurrence(xp, sgu_args, xs, hand, state, w, *, seg):
    e, dh = D_INNER, HEAD_DIM
    c0, n0, m0 = state
    n_streams = c0.shape[0]
    groups = HEADS // HEAD_GROUP
    tm = MXU_DIM
    steps = xp.shape[0] // tm
    assert steps == n_streams * groups, (steps, n_streams, groups)
    gw = HEAD_GROUP * dh
    norm_g, w_in, ln_g, ln_b, w_s, b_s, w_out = sgu_args

    def stream(i):
        return i // groups

    def group(i):
        return i % groups

    sgu_specs = [pl.BlockSpec((tm, D_MODEL), lambda i: (i, 0)),
                 _const_spec((1, D_MODEL)), _const_spec(w_in.shape),
                 _const_spec((1, e)), _const_spec((1, e)),
                 _const_spec((SGU_GROUPS, SGU_CHUNK, SGU_CHUNK)),
                 _const_spec((SGU_CHUNK, SGU_GROUPS)), _const_spec(w_out.shape)]
    hand_specs = [pl.BlockSpec((seg, gw), lambda i: (stream(i), group(i))) if width == e
                  else pl.BlockSpec((seg, width), lambda i: (stream(i), 0))
                  for _, width, _ in _HANDOFF]
    c_spec = pl.BlockSpec((1, HEAD_GROUP, dh, dh), lambda i: (stream(i), group(i), 0, 0))
    n_spec = pl.BlockSpec((1, 1, HEAD_GROUP, dh), lambda i: (stream(i), group(i), 0, 0))
    m_spec = pl.BlockSpec((1, 1, LANES), lambda i: (stream(i), 0, 0))
    ys_spec = pl.BlockSpec((seg, D_MODEL), lambda i: (stream(i), 0))
    head_vec_spec = pl.BlockSpec((1, gw), lambda i: (0, group(i)))
    wout_spec = pl.BlockSpec((D_MODEL // MXU_DIM, gw, MXU_DIM), lambda i: (0, group(i), 0))
    n0g = n0.reshape(n_streams, groups, HEAD_GROUP, dh)
    xp1, ys, c_out, n_out, m_out = pl.pallas_call(
        functools.partial(_sgu_recurrence_kernel, tm=tm, seg=seg),
        grid=(steps,),
        in_specs=(sgu_specs + [ys_spec] + hand_specs + [c_spec, n_spec, m_spec]
                  + [head_vec_spec, head_vec_spec, wout_spec, _const_spec((1, D_MODEL))]),
        out_specs=[pl.BlockSpec((tm, D_MODEL), lambda i: (i, 0)), ys_spec,
                   c_spec, n_spec, m_spec],
        out_shape=[jax.ShapeDtypeStruct(xp.shape, F32), jax.ShapeDtypeStruct(xs.shape, F32),
                   jax.ShapeDtypeStruct(c0.shape, F32), jax.ShapeDtypeStruct(n0g.shape, F32),
                   jax.ShapeDtypeStruct(m0.shape, F32)],
        scratch_shapes=[pltpu.VMEM((seg, D_MODEL), F32)],
        compiler_params=pltpu.CompilerParams(
            dimension_semantics=("arbitrary",), vmem_limit_bytes=VMEM_LIMIT_BYTES),
        name="sgu_prompt_sample_recurrence",
    )(xp, norm_g, w_in, ln_g, ln_b, w_s, b_s, w_out, xs, *hand, c0, n0g, m0,
      w['hn_g'], w['skip'], w['w_out'], w['final_g'])
    return xp1, ys, c_out, n_out.reshape(n0.shape), m_out


def _expand_blockdiag(w):
    nt = D_INNER // MXU_DIM
    rows = w.reshape(3, nt, MXU_DIM, QKV_BLOCK)
    tiled = jnp.tile(rows, (1, 1, 1, MXU_DIM // QKV_BLOCK))
    blk = np.arange(MXU_DIM) // QKV_BLOCK
    return jnp.where(blk[:, None] == blk[None, :], tiled, 0.0).astype(BF16)


def _fold_kernel(bd_ref, wg_ref, o_ref):
    for j in range(bd_ref.shape[1]):
        r0, r1 = j * MXU_DIM, (j + 1) * MXU_DIM
        o_ref[0, r0:r1, :] = (_dot(bd_ref[0, j], wg_ref[0, r0:r1, :])
                              + _dot(bd_ref[1, j], wg_ref[1, r0:r1, :])).astype(BF16)
        o_ref[1, r0:r1, :] = _dot(bd_ref[2, j], wg_ref[2, r0:r1, :]).astype(BF16)


def _fold_gate_weights(bd, wg):
    return pl.pallas_call(
        _fold_kernel,
        out_shape=jax.ShapeDtypeStruct((2, D_INNER, wg.shape[-1]), BF16),
        name="fold_gate_weights",
    )(bd, wg)


def _head_lanes(w):
    lead = w.shape[:-1]
    z = jnp.zeros(lead + (LANES - HEADS,), w.dtype)
    return jnp.concatenate([w[..., :HEADS], z, w[..., HEADS:], z], axis=-1)


def kernel(x_prompt, x_sample, state_mlstm_C, state_mlstm_n, state_mlstm_m, state_mlstm_conv, norm_g, final_norm_g, a_w_in, a_ln_g, a_ln_b, a_w_s, a_b_s, a_w_out, b_w_in, b_conv_w, b_conv_b, b_wq, b_wk, b_wv, b_w_gates, b_b_gates, b_hnorm_g, b_skip, b_w_out):
    bsz, seq, d = x_prompt.shape
    dec_b, dec_t, _ = x_sample.shape
    e = D_INNER
    assert norm_g.shape[0] == 2 and a_w_in.shape[0] == 1 and b_w_in.shape[0] == 1
    assert d == D_MODEL and seq % MXU_DIM == 0 and SGU_CHUNK % dec_t == 0

    xp = x_prompt.reshape(bsz * seq, d)
    xs = x_sample.reshape(dec_b * dec_t, d)

    a_in = _col_slabs(a_w_in[0])
    a_out = _col_slabs(a_w_out[0])
    ng0 = norm_g[0].reshape(1, d)
    lng = a_ln_g[0].reshape(1, e)
    lnb = a_ln_b[0].reshape(1, e)
    per_tile = SGU_CHUNK // dec_t
    w_head = a_w_s[0][:, :dec_t, :dec_t]
    ws_s = jnp.einsum('ab,gts->gatbs', np.eye(per_tile, dtype=np.float32), w_head)
    ws_s = ws_s.reshape(SGU_GROUPS, SGU_CHUNK, SGU_CHUNK)
    bs_s = jnp.tile(a_b_s[0][:, :dec_t], (1, per_tile)).T
    xs1, v_s = _sgu_layer(xs, ng0, a_in, lng, lnb, ws_s, bs_s, a_out,
                          tm=2 * SGU_CHUNK, want_v=True)

    bd = _expand_blockdiag(jnp.stack([b_wq[0], b_wk[0], b_wv[0]]))
    wg = _head_lanes(b_w_gates[0]).astype(BF16).reshape(3, e, 2 * LANES)
    w = dict(norm_g=norm_g[1].reshape(1, d), final_g=final_norm_g.reshape(1, d),
             w_in=_col_slabs(b_w_in[0]), conv_w=b_conv_w[0], conv_b=b_conv_b[0].reshape(1, e),
             bdq=bd[0], bdk=bd[1], bdv=bd[2],
             wg=_fold_gate_weights(bd, wg),
             bg=_head_lanes(b_b_gates[0]).reshape(1, 2 * LANES),
             hn_g=b_hnorm_g[0].reshape(1, e), skip=b_skip[0].reshape(1, e),
             w_out=_col_slabs(b_w_out[0]))
    hand, conv_s = _mlstm_sample_front(xs1, w, state_mlstm_conv[0], seg=dec_t,
                                       ns_front=dec_b // 2)
    m0 = jnp.pad(state_mlstm_m[0], ((0, 0), (0, LANES - HEADS))).reshape(dec_b, 1, LANES)
    xp1, ys, c_s, n_s, m_s = _sgu_prompt_with_sample_recurrence(
        xp, (ng0, a_in, lng, lnb, a_w_s[0], a_b_s[0].T, a_out), xs1, hand,
        (state_mlstm_C[0], state_mlstm_n[0], m0), w, seg=dec_t)
    yp, c_p, n_p, m_p, conv_p = _mlstm_prompt(xp1, w, n_streams=bsz, seg=MXU_DIM)

    return (yp.reshape(bsz, seq, d),
            ys.reshape(dec_b, dec_t, d),
            v_s.reshape(1, dec_b, dec_t, e),
            c_p[None], n_p[None], m_p[:, 0, :HEADS][None], conv_p[None],
            c_s[None], n_s[None], m_s[:, 0, :HEADS][None], conv_s[None])
```

```python
import functools
import math

import numpy as np
import jax
import jax.numpy as jnp
from jax import lax
from jax.experimental import pallas as pl
from jax.experimental.pallas import tpu as pltpu

D_MODEL = 1024
D_INNER = 2048
SGU_CHUNK = 128
SGU_GROUPS = 8
SGU_GROUP_DIM = D_INNER // SGU_GROUPS
HEADS = 4
HEAD_DIM = D_INNER // HEADS
QKV_BLOCK = 4
CONV_W = 4
RMS_EPS = 1e-6
LN_EPS = 1e-5

LANES = 128
SUBLANES = 8
MXU_DIM = 256
VMEM_LIMIT_BYTES = 60 * 1024 * 1024
SLAB_BLOCK_BYTES = 6 * 1024 * 1024

BF16 = jnp.bfloat16
F32 = jnp.float32
LOG2E = math.log2(math.e)

assert SGU_GROUP_DIM == MXU_DIM


def _dot(a, b):
    return jnp.dot(a, b, preferred_element_type=F32)


def _rmsnorm(x, g):
    return x * lax.rsqrt(jnp.mean(x * x, axis=-1, keepdims=True) + RMS_EPS) * g


def _log_sigmoid(x):
    return jnp.minimum(x, 0.0) - jnp.log1p(jnp.exp(-jnp.abs(x)))


def _prefix_rows(x, op, fill):
    n = x.shape[0]
    row = lax.broadcasted_iota(jnp.int32, x.shape, 0)
    shift = 1
    while shift < n:
        if shift % SUBLANES == 0:
            shifted = jnp.concatenate(
                [jnp.full((shift, x.shape[1]), fill, x.dtype), x[:n - shift]], axis=0)
        else:
            shifted = jnp.where(row >= shift, pltpu.roll(x, shift, 0), fill)
        x = op(x, shifted)
        shift *= 2
    return x


def _const_spec(shape):
    zeros = (0,) * len(shape)
    return pl.BlockSpec(shape, lambda *_: zeros, pipeline_mode=pl.Buffered(1))


def _slab_kernel(w_ref, o_ref):
    for j in range(o_ref.shape[0]):
        o_ref[j] = w_ref[:, j * MXU_DIM:(j + 1) * MXU_DIM].astype(BF16)


def _col_slabs(w):
    k, n = w.shape
    rows = MXU_DIM
    while 2 * rows <= k and 2 * rows * n * 4 <= SLAB_BLOCK_BYTES:
        rows *= 2
    return pl.pallas_call(
        _slab_kernel,
        grid=(k // rows,),
        in_specs=[pl.BlockSpec((rows, n), lambda i: (i, 0))],
        out_specs=pl.BlockSpec((n // MXU_DIM, rows, MXU_DIM), lambda i: (0, i, 0)),
        out_shape=jax.ShapeDtypeStruct((n // MXU_DIM, k, MXU_DIM), BF16),
        compiler_params=pltpu.CompilerParams(dimension_semantics=("arbitrary",)),
        name="weight_slabs",
    )(w)


def _sgu_kernel(x_ref, ng_ref, win_ref, lng_ref, lnb_ref, ws_ref, bs_ref, wout_ref,
                y_ref, *v_refs, tm):
    _interleave(_sgu_stages(x_ref, ng_ref, win_ref, lng_ref, lnb_ref, ws_ref, bs_ref, wout_ref,
                            y_ref, v_refs, tm=tm))


def _sgu_stages(x_ref, ng_ref, win_ref, lng_ref, lnb_ref, ws_ref, bs_ref, wout_ref,
                y_ref, v_refs, *, tm):
    ng = SGU_GROUPS
    e = D_INNER
    x = x_ref[...]
    xn = _rmsnorm(x, ng_ref[...]).astype(BF16)

    v_g, uz_g = [], []
    row_sum = jnp.zeros((tm, 1), F32)
    for g in range(ng):
        v = jax.nn.gelu(_dot(xn, win_ref[ng + g]))
        row_sum = row_sum + jnp.sum(v, axis=1, keepdims=True)
        v_g.append(v)
        if g % 4 == 3:
            yield
    mu = row_sum * (1.0 / e)
    sq_sum = jnp.zeros((tm, 1), F32)
    for g in range(ng):
        u = jax.nn.gelu(_dot(xn, win_ref[g]))
        z = _dot(xn, win_ref[2 * ng + g])
        uz_g.append(u * jax.nn.silu(z))
        v_g[g] = v_g[g] - mu
        sq_sum = sq_sum + jnp.sum(v_g[g] * v_g[g], axis=1, keepdims=True)
        if g % 4 == 3:
            yield
    rstd = lax.rsqrt(sq_sum * (1.0 / e) + LN_EPS)

    ti = lax.broadcasted_iota(jnp.int32, (SGU_CHUNK, SGU_CHUNK), 0)
    si = lax.broadcasted_iota(jnp.int32, (SGU_CHUNK, SGU_CHUNK), 1)
    causal = si <= ti
    s_g = []
    for g in range(ng):
        lo, hi = g * SGU_GROUP_DIM, (g + 1) * SGU_GROUP_DIM
        vn = v_g[g] * rstd * lng_ref[:, lo:hi] + lnb_ref[:, lo:hi]
        if v_refs:
            v_refs[0][:, lo:hi] = vn
        vb = vn.astype(BF16)
        wm = jnp.where(causal, ws_ref[g], 0.0).astype(BF16)
        bias = bs_ref[:, g:g + 1]
        s_g.append(jnp.concatenate(
            [_dot(wm, vb[c * SGU_CHUNK:(c + 1) * SGU_CHUNK]) + bias
             for c in range(tm // SGU_CHUNK)], axis=0))
    yield
    n_out = D_MODEL // MXU_DIM
    acc = [None] * n_out
    out_next = (uz_g[0] * s_g[0]).astype(BF16)
    for g in range(ng):
        lo, hi = g * SGU_GROUP_DIM, (g + 1) * SGU_GROUP_DIM
        out = out_next
        if g + 1 < ng:
            out_next = (uz_g[g + 1] * s_g[g + 1]).astype(BF16)
        for n in range(n_out):
            part = _dot(out, wout_ref[n, lo:hi, :])
            acc[n] = part if acc[n] is None else acc[n] + part
        if g == ng // 2 - 1:
            yield
    y_ref[...] = x + jnp.concatenate(acc, axis=1)


def _sgu_layer(x, norm_g, w_in, ln_g, ln_b, w_s, b_s, w_out, *, tm, want_v):
    rows = x.shape[0]
    e = D_INNER
    row_spec = pl.BlockSpec((tm, D_MODEL), lambda i: (i, 0))
    out_shape = [jax.ShapeDtypeStruct((rows, D_MODEL), F32)]
    out_specs = [row_spec]
    if want_v:
        out_shape.append(jax.ShapeDtypeStruct((rows, e), F32))
        out_specs.append(pl.BlockSpec((tm, e), lambda i: (i, 0)))
    res = pl.pallas_call(
        functools.partial(_sgu_kernel, tm=tm),
        grid=(rows // tm,),
        in_specs=[row_spec,
                  _const_spec((1, D_MODEL)),
                  _const_spec(w_in.shape),
                  _const_spec((1, e)),
                  _const_spec((1, e)),
                  _const_spec((SGU_GROUPS, SGU_CHUNK, SGU_CHUNK)),
                  _const_spec((SGU_CHUNK, SGU_GROUPS)),
                  _const_spec(w_out.shape)],
        out_specs=out_specs,
        out_shape=out_shape,
        compiler_params=pltpu.CompilerParams(
            dimension_semantics=("arbitrary",), vmem_limit_bytes=VMEM_LIMIT_BYTES),
        name="sgu_layer_v" if want_v else "sgu_layer",
    )(x, norm_g, w_in, ln_g, ln_b, w_s, b_s, w_out)
    return res


def _cols_to_rows(a, seg, heads):
    if seg % LANES == 0:
        return a.T[0:SUBLANES, :]
    ti = lax.broadcasted_iota(jnp.int32, (seg, seg), 0)
    si = lax.broadcasted_iota(jnp.int32, (seg, seg), 1)
    return jnp.concatenate(
        [jnp.sum(jnp.where(ti == si, a[:, h:h + 1], 0.0), axis=0, keepdims=True)
         for h in range(heads)], axis=0)


_HANDOFF = (('xc', D_INNER, F32), ('sz', D_INNER, F32), ('qb', D_INNER, BF16),
            ('ks', D_INNER, F32), ('vb', D_INNER, BF16), ('gates', 2 * LANES, F32))
_HANDOFF_NAMES = tuple(name for name, _, _ in _HANDOFF)


def _interleave(*gens):
    live = list(gens)
    while live:
        for gen in list(live):
            try:
                next(gen)
            except StopIteration:
                live.remove(gen)


def _frontend(x_ref, ng_ref, win_ref, cw_ref, cb_ref, bdq_ref, bdk_ref, bdv_ref, wg_ref,
              bg_ref, xpad_ref, dst, *, ns, seg):
    e, dh = D_INNER, HEAD_DIM
    pad = SUBLANES
    rows = ns * seg
    tiles_per_head = dh // MXU_DIM
    xn = _rmsnorm(x_ref[...], ng_ref[...]).astype(BF16)

    def project(h):
        t0 = h * tiles_per_head
        xm_ = jnp.concatenate(
            [_dot(xn, win_ref[t0 + j]) for j in range(tiles_per_head)], axis=1)
        z_ = jnp.concatenate(
            [_dot(xn, win_ref[e // MXU_DIM + t0 + j]) for j in range(tiles_per_head)], axis=1)
        return xm_, z_

    projected = project(0)
    gates = bg_ref[...]
    yield
    for h in range(HEADS):
        c0, c1 = h * dh, (h + 1) * dh
        t0 = h * tiles_per_head
        xm, z = projected
        if h + 1 < HEADS:
            projected = project(h + 1)
        xpad_ref[:, pad:pad + seg, c0:c1] = xm.reshape(ns, seg, dh)
        pre = cb_ref[:, c0:c1] + cw_ref[CONV_W - 1:CONV_W, c0:c1] * xm
        for d in range(1, CONV_W):
            shifted = xpad_ref[:, pad - d:pad - d + seg, c0:c1].reshape(rows, dh)
            pre = pre + cw_ref[CONV_W - 1 - d:CONV_W - d, c0:c1] * shifted
        xc = jax.nn.silu(pre)
        xcb = xc.astype(BF16)
        xmb = xm.astype(BF16)

        def blockdiag(a, w_ref):
            return jnp.concatenate(
                [_dot(a[:, j * MXU_DIM:(j + 1) * MXU_DIM], w_ref[t0 + j])
                 for j in range(tiles_per_head)], axis=1)

        q = blockdiag(xcb, bdq_ref)
        k = blockdiag(xcb, bdk_ref)
        v = blockdiag(xmb, bdv_ref)
        qb, vb = q.astype(BF16), v.astype(BF16)
        gates = gates + _dot(xcb, wg_ref[0, c0:c1, :]) + _dot(xmb, wg_ref[1, c0:c1, :])
        dst['xc'][:, c0:c1] = xc
        dst['sz'][:, c0:c1] = jax.nn.silu(z)
        dst['qb'][:, c0:c1] = qb
        dst['ks'][:, c0:c1] = k * (dh ** -0.5)
        dst['vb'][:, c0:c1] = vb
        if h == HEADS - 1:
            dst['gates'][...] = gates
        yield


def _backend(xres_ref, src, cin_ref, nin_ref, min_ref, hng_ref, skip_ref, wout_ref, fg_ref,
             y_ref, c_ref, n_ref, m_ref, *, ns, seg, heads=HEADS, head0=None, acc_ref=None,
             memory_read_first=False):
    dh = HEAD_DIM
    n_out = D_MODEL // MXU_DIM
    ti = lax.broadcasted_iota(jnp.int32, (seg, seg), 0)
    si = lax.broadcasted_iota(jnp.int32, (seg, seg), 1)
    causal = si <= ti
    pairs = [(s, h) for s in range(ns) for h in range(heads)]

    def block(name, s, h):
        return src[name][s * seg:(s + 1) * seg, h * dh:(h + 1) * dh]

    def local(x):
        if head0 is None:
            return x
        return pltpu.roll(x, (LANES - head0) % LANES, 1)

    qc, cprev = {}, {}
    if memory_read_first:
        qc = {(s, h): _dot(block('qb', s, h), cin_ref[s, h].astype(BF16)) for s, h in pairs}
    gt = []
    for s in range(ns):
        g = src['gates'][s * seg:(s + 1) * seg, :]
        ig = g[:, 0:LANES]
        lf = _log_sigmoid(g[:, LANES:2 * LANES])
        b = _prefix_rows(lf, jnp.add, 0.0)
        a = ig - b
        cm = _prefix_rows(a, jnp.maximum, -jnp.inf)
        m_prev = min_ref[s]
        neg_mt_plus_b = -jnp.maximum(m_prev, cm)
        b_last = b[seg - 1:seg, :]
        m_new = b_last + jnp.maximum(m_prev, cm[seg - 1:seg, :])
        decay = jnp.broadcast_to(jnp.exp(b_last + m_prev - m_new), (SUBLANES, LANES))
        gt.append(dict(
            w_inter=local(jnp.exp(m_prev + neg_mt_plus_b)),
            inv_floor=local(jnp.exp(neg_mt_plus_b - b)),
            w=local(jnp.exp(b_last + a - m_new)),
            decay=local(decay)[0:1, :],
            a_rows=_cols_to_rows(local(a), seg, heads) * LOG2E,
            u=local(neg_mt_plus_b) * LOG2E,
            n_prev=nin_ref[s]))
        m_ref[s] = m_new
    yield

    sc = {}
    for s, h in pairs:
        w_intra = jnp.where(
            causal, jnp.exp2(gt[s]['u'][:, h:h + 1] + gt[s]['a_rows'][h:h + 1, :]), 0.0)
        sc[s, h] = lax.dot_general(block('qb', s, h), block('ks', s, h).astype(BF16),
                                   (((1,), (1,)), ((), ())),
                                   preferred_element_type=F32) * w_intra
        if not memory_read_first:
            cprev[s, h] = cin_ref[s, h]
            qc[s, h] = _dot(block('qb', s, h), cprev[s, h].astype(BF16))
    yield

    num = {(s, h): _dot(sc[s, h].astype(BF16), block('vb', s, h))
           + gt[s]['w_inter'][:, h:h + 1] * qc[s, h] for s, h in pairs}
    yield

    out, n_new = {}, {}
    for s, h in pairs:
        c0, c1 = h * dh, (h + 1) * dh
        n_prev = gt[s]['n_prev'][h:h + 1, :]
        qn = jnp.sum(block('qb', s, h).astype(F32) * n_prev, axis=1, keepdims=True)
        den = jnp.sum(sc[s, h], axis=1, keepdims=True) + gt[s]['w_inter'][:, h:h + 1] * qn
        r = 1.0 / jnp.maximum(jnp.abs(den), gt[s]['inv_floor'][:, h:h + 1])
        cen = num[s, h] - jnp.mean(num[s, h], axis=1, keepdims=True)
        var = jnp.mean(cen * cen, axis=1, keepdims=True)
        hn = cen * (r * lax.rsqrt(r * r * var + LN_EPS)) * hng_ref[:, c0:c1]
        out[s, h] = ((hn + skip_ref[:, c0:c1] * block('xc', s, h))
                     * block('sz', s, h)).astype(BF16)

        decay = gt[s]['decay'][:, h:h + 1]
        kw = block('ks', s, h) * gt[s]['w'][:, h:h + 1]
        upd = lax.dot_general(kw.astype(BF16), block('vb', s, h), (((0,), (0,)), ((), ())),
                              preferred_element_type=F32)
        c_old = cin_ref[s, h] if memory_read_first else cprev[s, h]
        c_ref[s, h] = decay * c_old + upd
        n_new[s, h] = decay * n_prev + jnp.sum(kw, axis=0, keepdims=True)
    for s in range(ns):
        n_ref[s] = jnp.concatenate([n_new[s, h] for h in range(heads)], axis=0)
    yield

    y_rows = []
    for s in range(ns):
        acc = [None] * n_out
        for h in range(heads):
            for n in range(n_out):
                part = _dot(out[s, h], wout_ref[n, h * dh:(h + 1) * dh, :])
                acc[n] = part if acc[n] is None else acc[n] + part
        y_rows.append(jnp.concatenate(acc, axis=1))
    proj = y_rows[0] if ns == 1 else jnp.concatenate(y_rows, axis=0)
    if acc_ref is not None:
        proj = jnp.where(head0 == 0, 0.0, acc_ref[...]) + proj
        acc_ref[...] = proj
    y_ref[...] = _rmsnorm(xres_ref[...] + proj, fg_ref[...])


def _mlstm_pipelined_kernel(x_ref, xprev_ref, ng_ref, fg_ref, win_ref, cw_ref, cb_ref,
                            bdq_ref, bdk_ref, bdv_ref, wg_ref, bg_ref, hng_ref, skip_ref,
                            wout_ref, y_ref, c_ref, n_ref, m_ref, conv_ref, xpad_ref,
                            *handoff, seg, n_tiles):
    t = pl.program_id(0)
    e = D_INNER
    pad = SUBLANES
    n_hand = len(_HANDOFF)
    sets = (dict(zip(_HANDOFF_NAMES, handoff[:n_hand])),
            dict(zip(_HANDOFF_NAMES, handoff[n_hand:])))

    @pl.when(t == 0)
    def _():
        for ref in sets[1].values():
            ref[...] = jnp.zeros(ref.shape, ref.dtype)

    @pl.when(t % n_tiles == 0)
    def _():
        xpad_ref[:, 0:pad, :] = jnp.zeros((1, pad, e), F32)

    @pl.when((t == 0) | ((t - 1) % n_tiles == 0))
    def _():
        c_ref[...] = jnp.zeros(c_ref.shape, F32)
        n_ref[...] = jnp.zeros(n_ref.shape, F32)
        m_ref[...] = jnp.zeros(m_ref.shape, F32)

    def step(dst, src):
        front = _frontend(x_ref, ng_ref, win_ref, cw_ref, cb_ref, bdq_ref, bdk_ref, bdv_ref,
                          wg_ref, bg_ref, xpad_ref, dst, ns=1, seg=seg)
        back = _backend(xprev_ref, src, c_ref, n_ref, m_ref, hng_ref, skip_ref, wout_ref,
                        fg_ref, y_ref, c_ref, n_ref, m_ref, ns=1, seg=seg)
        _interleave(back, front)
        conv_ref[...] = xpad_ref[:, pad + seg - (CONV_W - 1):pad + seg, :]
        xpad_ref[:, 0:pad, :] = xpad_ref[:, seg:seg + pad, :]

    @pl.when(t % 2 == 0)
    def _():
        step(sets[0], sets[1])

    @pl.when(t % 2 == 1)
    def _():
        step(sets[1], sets[0])


def _mlstm_front_kernel(x_ref, ng_ref, win_ref, cw_ref, cb_ref, bdq_ref, bdk_ref, bdv_ref,
                        wg_ref, bg_ref, conv0_ref, *rest, ns, seg):
    n_hand = len(_HANDOFF)
    dst = dict(zip(_HANDOFF_NAMES, rest[:n_hand]))
    conv_ref, xpad_ref = rest[n_hand:]
    pad = SUBLANES
    xpad_ref[:, pad - (CONV_W - 1):pad, :] = conv0_ref[...]
    _interleave(_frontend(x_ref, ng_ref, win_ref, cw_ref, cb_ref, bdq_ref, bdk_ref, bdv_ref,
                          wg_ref, bg_ref, xpad_ref, dst, ns=ns, seg=seg))
    conv_ref[...] = xpad_ref[:, pad + seg - (CONV_W - 1):pad + seg, :]


HEAD_GROUP = 2


def _sgu_recurrence_kernel(*refs, tm, seg):
    n_hand = len(_HANDOFF)
    sgu_refs, rest = refs[:8], refs[8:]
    xres_ref = rest[0]
    src = dict(zip(_HANDOFF_NAMES, rest[1:1 + n_hand]))
    (c0_ref, n0_ref, m0_ref, hng_ref, skip_ref, wout_ref, fg_ref,
     y_ref, ys_ref, c_ref, n_ref, m_ref, acc_ref) = rest[1 + n_hand:]
    i = pl.program_id(0)
    head0 = (i % (HEADS // HEAD_GROUP)) * HEAD_GROUP

    @pl.when(i == 0)
    def _():
        acc_ref[...] = jnp.zeros(acc_ref.shape, F32)

    _interleave(
        _backend(xres_ref, src, c0_ref, n0_ref.at[0], m0_ref, hng_ref, skip_ref, wout_ref,
                 fg_ref, ys_ref, c_ref, n_ref.at[0], m_ref, ns=1, seg=seg,
                 heads=HEAD_GROUP, head0=head0, acc_ref=acc_ref, memory_read_first=True),
        _sgu_stages(*sgu_refs, y_ref, (), tm=tm))


def _state_specs(ns, index):
    dh = HEAD_DIM
    return [pl.BlockSpec((ns, HEADS, dh, dh), lambda *i: (index(*i), 0, 0, 0)),
            pl.BlockSpec((ns, HEADS, dh), lambda *i: (index(*i), 0, 0)),
            pl.BlockSpec((ns, 1, LANES), lambda *i: (index(*i), 0, 0))]


def _state_shapes(n_streams):
    dh = HEAD_DIM
    return [jax.ShapeDtypeStruct((n_streams, HEADS, dh, dh), F32),
            jax.ShapeDtypeStruct((n_streams, HEADS, dh), F32),
            jax.ShapeDtypeStruct((n_streams, 1, LANES), F32)]


def _front_weight_specs(w):
    nt = D_INNER // MXU_DIM
    e = D_INNER
    return [_const_spec((1, D_MODEL)), _const_spec(w['w_in'].shape),
            _const_spec((CONV_W, e)), _const_spec((1, e)),
            _const_spec((nt, MXU_DIM, MXU_DIM)), _const_spec((nt, MXU_DIM, MXU_DIM)),
            _const_spec((nt, MXU_DIM, MXU_DIM)),
            _const_spec((2, e, 2 * LANES)), _const_spec((1, 2 * LANES))]


def _front_weight_args(w):
    return [w['norm_g'], w['w_in'], w['conv_w'], w['conv_b'], w['bdq'], w['bdk'], w['bdv'],
            w['wg'], w['bg']]


def _mlstm_prompt(x, w, *, n_streams, seg):
    e = D_INNER
    n_tiles = x.shape[0] // (n_streams * seg)
    total = n_streams * n_tiles

    def front_tile(t):
        return jnp.minimum(t, total - 1)

    def back_tile(t):
        return jnp.maximum(t - 1, 0)

    cur_spec = pl.BlockSpec((seg, D_MODEL), lambda t: (front_tile(t), 0))
    prev_spec = pl.BlockSpec((seg, D_MODEL), lambda t: (back_tile(t), 0))
    conv_spec = pl.BlockSpec((1, CONV_W - 1, e), lambda t: (front_tile(t) // n_tiles, 0, 0))
    back_specs = [_const_spec((1, e)), _const_spec((1, e)), _const_spec(w['w_out'].shape)]
    fw = _front_weight_specs(w)
    in_specs = [cur_spec, prev_spec, fw[0], _const_spec((1, D_MODEL))] + fw[1:] + back_specs
    fa = _front_weight_args(w)
    args = [x, x, fa[0], w['final_g']] + fa[1:] + [w['hn_g'], w['skip'], w['w_out']]
    handoff = [pltpu.VMEM((seg, width), dtype)
               for _ in range(2) for _, width, dtype in _HANDOFF]
    return pl.pallas_call(
        functools.partial(_mlstm_pipelined_kernel, seg=seg, n_tiles=n_tiles),
        grid=(total + 1,),
        in_specs=in_specs,
        out_specs=([prev_spec] + _state_specs(1, lambda t: back_tile(t) // n_tiles)
                   + [conv_spec]),
        out_shape=([jax.ShapeDtypeStruct(x.shape, F32)] + _state_shapes(n_streams)
                   + [jax.ShapeDtypeStruct((n_streams, CONV_W - 1, e), F32)]),
        scratch_shapes=[pltpu.VMEM((1, SUBLANES + seg, e), F32)] + handoff,
        compiler_params=pltpu.CompilerParams(
            dimension_semantics=("arbitrary",),
            vmem_limit_bytes=VMEM_LIMIT_BYTES),
        name="mlstm_prompt",
    )(*args)


def _mlstm_sample_front(x, w, conv0, *, seg, ns_front):
    e = D_INNER
    n_streams = conv0.shape[0]
    rows_f = ns_front * seg
    hand_shapes = [jax.ShapeDtypeStruct((x.shape[0], width), dtype)
                   for _, width, dtype in _HANDOFF]
    conv_spec = pl.BlockSpec((ns_front, CONV_W - 1, e), lambda i: (i, 0, 0))
    res = pl.pallas_call(
        functools.partial(_mlstm_front_kernel, ns=ns_front, seg=seg),
        grid=(n_streams // ns_front,),
        in_specs=([pl.BlockSpec((rows_f, D_MODEL), lambda i: (i, 0))]
                  + _front_weight_specs(w) + [conv_spec]),
        out_specs=([pl.BlockSpec((rows_f, width), lambda i: (i, 0)) for _, width, _ in _HANDOFF]
                   + [conv_spec]),
        out_shape=hand_shapes + [jax.ShapeDtypeStruct((n_streams, CONV_W - 1, e), F32)],
        scratch_shapes=[pltpu.VMEM((ns_front, SUBLANES + seg, e), F32)],
        compiler_params=pltpu.CompilerParams(
            dimension_semantics=("arbitrary",), vmem_limit_bytes=VMEM_LIMIT_BYTES),
        name="mlstm_sample_front",
    )(x, *_front_weight_args(w), conv0)
    return res[:-1], res[-1]


def _sgu_prompt_with_sample_recurrence(xp, sgu_args, xs, hand, state, w, *, seg):
    e, dh = D_INNER, HEAD_DIM
    c0, n0, m0 = state
    n_streams = c0.shape[0]
    groups = HEADS // HEAD_GROUP
    tm = MXU_DIM
    steps = xp.shape[0] // tm
    assert steps == n_streams * groups, (steps, n_streams, groups)
    gw = HEAD_GROUP * dh
    norm_g, w_in, ln_g, ln_b, w_s, b_s, w_out = sgu_args

    def stream(i):
        return i // groups

    def group(i):
        return i % groups

    sgu_specs = [pl.BlockSpec((tm, D_MODEL), lambda i: (i, 0)),
                 _const_spec((1, D_MODEL)), _const_spec(w_in.shape),
                 _const_spec((1, e)), _const_spec((1, e)),
                 _const_spec((SGU_GROUPS, SGU_CHUNK, SGU_CHUNK)),
                 _const_spec((SGU_CHUNK, SGU_GROUPS)), _const_spec(w_out.shape)]
    hand_specs = [pl.BlockSpec((seg, gw), lambda i: (stream(i), group(i))) if width == e
                  else pl.BlockSpec((seg, width), lambda i: (stream(i), 0))
                  for _, width, _ in _HANDOFF]
    c_spec = pl.BlockSpec((1, HEAD_GROUP, dh, dh), lambda i: (stream(i), group(i), 0, 0))
    n_spec = pl.BlockSpec((1, 1, HEAD_GROUP, dh), lambda i: (stream(i), group(i), 0, 0))
    m_spec = pl.BlockSpec((1, 1, LANES), lambda i: (stream(i), 0, 0))
    ys_spec = pl.BlockSpec((seg, D_MODEL), lambda i: (stream(i), 0))
    head_vec_spec = pl.BlockSpec((1, gw), lambda i: (0, group(i)))
    wout_spec = pl.BlockSpec((D_MODEL // MXU_DIM, gw, MXU_DIM), lambda i: (0, group(i), 0))
    n0g = n0.reshape(n_streams, groups, HEAD_GROUP, dh)
    xp1, ys, c_out, n_out, m_out = pl.pallas_call(
        functools.partial(_sgu_recurrence_kernel, tm=tm, seg=seg),
        grid=(steps,),
        in_specs=(sgu_specs + [ys_spec] + hand_specs + [c_spec, n_spec, m_spec]
                  + [head_vec_spec, head_vec_spec, wout_spec, _const_spec((1, D_MODEL))]),
        out_specs=[pl.BlockSpec((tm, D_MODEL), lambda i: (i, 0)), ys_spec,
                   c_spec, n_spec, m_spec],
        out_shape=[jax.ShapeDtypeStruct(xp.shape, F32), jax.ShapeDtypeStruct(xs.shape, F32),
                   jax.ShapeDtypeStruct(c0.shape, F32), jax.ShapeDtypeStruct(n0g.shape, F32),
                   jax.ShapeDtypeStruct(m0.shape, F32)],
        scratch_shapes=[pltpu.VMEM((seg, D_MODEL), F32)],
        compiler_params=pltpu.CompilerParams(
            dimension_semantics=("arbitrary",), vmem_limit_bytes=VMEM_LIMIT_BYTES),
        name="sgu_prompt_sample_recurrence",
    )(xp, norm_g, w_in, ln_g, ln_b, w_s, b_s, w_out, xs, *hand, c0, n0g, m0,
      w['hn_g'], w['skip'], w['w_out'], w['final_g'])
    return xp1, ys, c_out, n_out.reshape(n0.shape), m_out


def _qkv_weights_kernel(rows_ref, wg_ref, bd_ref, fold_ref):
    nt = rows_ref.shape[1]
    r_blk = lax.broadcasted_iota(jnp.int32, (MXU_DIM, MXU_DIM), 0) // QKV_BLOCK
    c_idx = lax.broadcasted_iota(jnp.int32, (MXU_DIM, MXU_DIM), 1)
    same_block = r_blk == c_idx // QKV_BLOCK
    for j in range(nt):
        r0, r1 = j * MXU_DIM, (j + 1) * MXU_DIM
        tiles = []
        for a in range(3):
            rows = rows_ref[a, j]
            tiled = jnp.zeros((MXU_DIM, MXU_DIM), F32)
            for o in range(QKV_BLOCK):
                tiled = jnp.where(c_idx % QKV_BLOCK == o, rows[:, o:o + 1], tiled)
            tile = jnp.where(same_block, tiled, 0.0).astype(BF16)
            bd_ref[a, j] = tile
            tiles.append(tile)
        fold_ref[0, r0:r1, :] = (_dot(tiles[0], wg_ref[0, r0:r1, :])
                                 + _dot(tiles[1], wg_ref[1, r0:r1, :])).astype(BF16)
        fold_ref[1, r0:r1, :] = _dot(tiles[2], wg_ref[2, r0:r1, :]).astype(BF16)


def _qkv_weights(wq, wk, wv, wg):
    nt = D_INNER // MXU_DIM
    rows = jnp.stack([wq, wk, wv]).reshape(3, nt, MXU_DIM, QKV_BLOCK)
    return pl.pallas_call(
        _qkv_weights_kernel,
        out_shape=[jax.ShapeDtypeStruct((3, nt, MXU_DIM, MXU_DIM), BF16),
                   jax.ShapeDtypeStruct((2, D_INNER, wg.shape[-1]), BF16)],
        name="qkv_weights",
    )(rows, wg)


def _head_lanes(w):
    lead = w.shape[:-1]
    z = jnp.zeros(lead + (LANES - HEADS,), w.dtype)
    return jnp.concatenate([w[..., :HEADS], z, w[..., HEADS:], z], axis=-1)


def kernel(x_prompt, x_sample, state_mlstm_C, state_mlstm_n, state_mlstm_m, state_mlstm_conv, norm_g, final_norm_g, a_w_in, a_ln_g, a_ln_b, a_w_s, a_b_s, a_w_out, b_w_in, b_conv_w, b_conv_b, b_wq, b_wk, b_wv, b_w_gates, b_b_gates, b_hnorm_g, b_skip, b_w_out):
    bsz, seq, d = x_prompt.shape
    dec_b, dec_t, _ = x_sample.shape
    e = D_INNER
    assert norm_g.shape[0] == 2 and a_w_in.shape[0] == 1 and b_w_in.shape[0] == 1
    assert d == D_MODEL and seq % MXU_DIM == 0 and SGU_CHUNK % dec_t == 0

    xp = x_prompt.reshape(bsz * seq, d)
    xs = x_sample.reshape(dec_b * dec_t, d)

    a_in = _col_slabs(a_w_in[0])
    a_out = _col_slabs(a_w_out[0])
    ng0 = norm_g[0].reshape(1, d)
    lng = a_ln_g[0].reshape(1, e)
    lnb = a_ln_b[0].reshape(1, e)
    per_tile = SGU_CHUNK // dec_t
    w_head = a_w_s[0][:, :dec_t, :dec_t]
    ws_s = jnp.einsum('ab,gts->gatbs', np.eye(per_tile, dtype=np.float32), w_head)
    ws_s = ws_s.reshape(SGU_GROUPS, SGU_CHUNK, SGU_CHUNK)
    bs_s = jnp.tile(a_b_s[0][:, :dec_t], (1, per_tile)).T
    xs1, v_s = _sgu_layer(xs, ng0, a_in, lng, lnb, ws_s, bs_s, a_out,
                          tm=2 * SGU_CHUNK, want_v=True)

    wg = _head_lanes(b_w_gates[0]).astype(BF16).reshape(3, e, 2 * LANES)
    bd, wg_folded = _qkv_weights(b_wq[0], b_wk[0], b_wv[0], wg)
    w = dict(norm_g=norm_g[1].reshape(1, d), final_g=final_norm_g.reshape(1, d),
             w_in=_col_slabs(b_w_in[0]), conv_w=b_conv_w[0], conv_b=b_conv_b[0].reshape(1, e),
             bdq=bd[0], bdk=bd[1], bdv=bd[2],
             wg=wg_folded,
             bg=_head_lanes(b_b_gates[0]).reshape(1, 2 * LANES),
             hn_g=b_hnorm_g[0].reshape(1, e), skip=b_skip[0].reshape(1, e),
             w_out=_col_slabs(b_w_out[0]))
    hand, conv_s = _mlstm_sample_front(xs1, w, state_mlstm_conv[0], seg=dec_t,
                                       ns_front=dec_b // 2)
    m0 = jnp.pad(state_mlstm_m[0], ((0, 0), (0, LANES - HEADS))).reshape(dec_b, 1, LANES)
    xp1, ys, c_s, n_s, m_s = _sgu_prompt_with_sample_recurrence(
        xp, (ng0, a_in, lng, lnb, a_w_s[0], a_b_s[0].T, a_out), xs1, hand,
        (state_mlstm_C[0], state_mlstm_n[0], m0), w, seg=dec_t)
    yp, c_p, n_p, m_p, conv_p = _mlstm_prompt(xp1, w, n_streams=bsz, seg=MXU_DIM)

    return (yp.reshape(bsz, seq, d),
            ys.reshape(dec_b, dec_t, d),
            v_s.reshape(1, dec_b, dec_t, e),
            c_p[None], n_p[None], m_p[:, 0, :HEADS][None], conv_p[None],
            c_s[None], n_s[None], m_s[:, 0, :HEADS][None], conv_s[None])
```

```python
import functools
import math

import numpy as np
import jax
import jax.numpy as jnp
from jax import lax
from jax.experimental import pallas as pl
from jax.experimental.pallas import tpu as pltpu

D_MODEL = 1024
D_INNER = 2048
SGU_CHUNK = 128
SGU_GROUPS = 8
SGU_GROUP_DIM = D_INNER // SGU_GROUPS
HEADS = 4
HEAD_DIM = D_INNER // HEADS
QKV_BLOCK = 4
CONV_W = 4
RMS_EPS = 1e-6
LN_EPS = 1e-5

LANES = 128
SUBLANES = 8
MXU_DIM = 256
VMEM_LIMIT_BYTES = 60 * 1024 * 1024
SLAB_BLOCK_BYTES = 6 * 1024 * 1024

BF16 = jnp.bfloat16
F32 = jnp.float32
LOG2E = math.log2(math.e)

assert SGU_GROUP_DIM == MXU_DIM


def _dot(a, b):
    return jnp.dot(a, b, preferred_element_type=F32)


def _rmsnorm(x, g):
    return x * lax.rsqrt(jnp.mean(x * x, axis=-1, keepdims=True) + RMS_EPS) * g


def _log_sigmoid(x):
    return jnp.minimum(x, 0.0) - jnp.log1p(jnp.exp(-jnp.abs(x)))


def _prefix_rows(x, op, fill):
    n = x.shape[0]
    row = lax.broadcasted_iota(jnp.int32, x.shape, 0)
    shift = 1
    while shift < n:
        if shift % SUBLANES == 0:
            shifted = jnp.concatenate(
                [jnp.full((shift, x.shape[1]), fill, x.dtype), x[:n - shift]], axis=0)
        else:
            shifted = jnp.where(row >= shift, pltpu.roll(x, shift, 0), fill)
        x = op(x, shifted)
        shift *= 2
    return x


def _const_spec(shape):
    zeros = (0,) * len(shape)
    return pl.BlockSpec(shape, lambda *_: zeros, pipeline_mode=pl.Buffered(1))


def _slab_kernel(w_ref, o_ref):
    for j in range(o_ref.shape[0]):
        o_ref[j] = w_ref[:, j * MXU_DIM:(j + 1) * MXU_DIM].astype(BF16)


def _col_slabs(w):
    k, n = w.shape
    rows = MXU_DIM
    while 2 * rows <= k and 2 * rows * n * 4 <= SLAB_BLOCK_BYTES:
        rows *= 2
    return pl.pallas_call(
        _slab_kernel,
        grid=(k // rows,),
        in_specs=[pl.BlockSpec((rows, n), lambda i: (i, 0))],
        out_specs=pl.BlockSpec((n // MXU_DIM, rows, MXU_DIM), lambda i: (0, i, 0)),
        out_shape=jax.ShapeDtypeStruct((n // MXU_DIM, k, MXU_DIM), BF16),
        compiler_params=pltpu.CompilerParams(dimension_semantics=("arbitrary",)),
        name="weight_slabs",
    )(w)


def _sgu_kernel(x_ref, ng_ref, win_ref, lng_ref, lnb_ref, ws_ref, bs_ref, wout_ref,
                y_ref, *v_refs, tm):
    _interleave(_sgu_stages(x_ref, ng_ref, win_ref, lng_ref, lnb_ref, ws_ref, bs_ref, wout_ref,
                            y_ref, v_refs, tm=tm))


def _sgu_stages(x_ref, ng_ref, win_ref, lng_ref, lnb_ref, ws_ref, bs_ref, wout_ref,
                y_ref, v_refs, *, tm):
    ng = SGU_GROUPS
    e = D_INNER
    x = x_ref[...]
    xn = _rmsnorm(x, ng_ref[...]).astype(BF16)

    v_g, uz_g = [], []
    row_sum = jnp.zeros((tm, 1), F32)
    for g in range(ng):
        v = jax.nn.gelu(_dot(xn, win_ref[ng + g]))
        row_sum = row_sum + jnp.sum(v, axis=1, keepdims=True)
        v_g.append(v)
        if g % 4 == 3:
            yield
    mu = row_sum * (1.0 / e)
    sq_sum = jnp.zeros((tm, 1), F32)
    for g in range(ng):
        u = jax.nn.gelu(_dot(xn, win_ref[g]))
        z = _dot(xn, win_ref[2 * ng + g])
        uz_g.append(u * jax.nn.silu(z))
        v_g[g] = v_g[g] - mu
        sq_sum = sq_sum + jnp.sum(v_g[g] * v_g[g], axis=1, keepdims=True)
        if g % 4 == 3:
            yield
    rstd = lax.rsqrt(sq_sum * (1.0 / e) + LN_EPS)

    ti = lax.broadcasted_iota(jnp.int32, (SGU_CHUNK, SGU_CHUNK), 0)
    si = lax.broadcasted_iota(jnp.int32, (SGU_CHUNK, SGU_CHUNK), 1)
    causal = si <= ti
    s_g = []
    for g in range(ng):
        lo, hi = g * SGU_GROUP_DIM, (g + 1) * SGU_GROUP_DIM
        vn = v_g[g] * rstd * lng_ref[:, lo:hi] + lnb_ref[:, lo:hi]
        if v_refs:
            v_refs[0][:, lo:hi] = vn
        vb = vn.astype(BF16)
        wm = jnp.where(causal, ws_ref[g], 0.0).astype(BF16)
        bias = bs_ref[:, g:g + 1]
        s_g.append(jnp.concatenate(
            [_dot(wm, vb[c * SGU_CHUNK:(c + 1) * SGU_CHUNK]) + bias
             for c in range(tm // SGU_CHUNK)], axis=0))
    yield
    n_out = D_MODEL // MXU_DIM
    acc = [None] * n_out
    out_next = (uz_g[0] * s_g[0]).astype(BF16)
    for g in range(ng):
        lo, hi = g * SGU_GROUP_DIM, (g + 1) * SGU_GROUP_DIM
        out = out_next
        if g + 1 < ng:
            out_next = (uz_g[g + 1] * s_g[g + 1]).astype(BF16)
        for n in range(n_out):
            part = _dot(out, wout_ref[n, lo:hi, :])
            acc[n] = part if acc[n] is None else acc[n] + part
        if g == ng // 2 - 1:
            yield
    y_ref[...] = x + jnp.concatenate(acc, axis=1)


def _sgu_layer(x, norm_g, w_in, ln_g, ln_b, w_s, b_s, w_out, *, tm, want_v):
    rows = x.shape[0]
    e = D_INNER
    row_spec = pl.BlockSpec((tm, D_MODEL), lambda i: (i, 0))
    out_shape = [jax.ShapeDtypeStruct((rows, D_MODEL), F32)]
    out_specs = [row_spec]
    if want_v:
        out_shape.append(jax.ShapeDtypeStruct((rows, e), F32))
        out_specs.append(pl.BlockSpec((tm, e), lambda i: (i, 0)))
    res = pl.pallas_call(
        functools.partial(_sgu_kernel, tm=tm),
        grid=(rows // tm,),
        in_specs=[row_spec,
                  _const_spec((1, D_MODEL)),
                  _const_spec(w_in.shape),
                  _const_spec((1, e)),
                  _const_spec((1, e)),
                  _const_spec((SGU_GROUPS, SGU_CHUNK, SGU_CHUNK)),
                  _const_spec((SGU_CHUNK, SGU_GROUPS)),
                  _const_spec(w_out.shape)],
        out_specs=out_specs,
        out_shape=out_shape,
        compiler_params=pltpu.CompilerParams(
            dimension_semantics=("arbitrary",), vmem_limit_bytes=VMEM_LIMIT_BYTES),
        name="sgu_layer_v" if want_v else "sgu_layer",
    )(x, norm_g, w_in, ln_g, ln_b, w_s, b_s, w_out)
    return res


def _cols_to_rows(a, seg, heads):
    if seg % LANES == 0:
        return a.T[0:SUBLANES, :]
    ti = lax.broadcasted_iota(jnp.int32, (seg, seg), 0)
    si = lax.broadcasted_iota(jnp.int32, (seg, seg), 1)
    return jnp.concatenate(
        [jnp.sum(jnp.where(ti == si, a[:, h:h + 1], 0.0), axis=0, keepdims=True)
         for h in range(heads)], axis=0)


_HANDOFF = (('xc', D_INNER, F32), ('sz', D_INNER, F32), ('qb', D_INNER, BF16),
            ('ks', D_INNER, F32), ('vb', D_INNER, BF16), ('gates', 2 * LANES, F32))
_HANDOFF_NAMES = tuple(name for name, _, _ in _HANDOFF)


def _interleave(*gens):
    live = list(gens)
    while live:
        for gen in list(live):
            try:
                next(gen)
            except StopIteration:
                live.remove(gen)


def _frontend(x_ref, ng_ref, win_ref, cw_ref, cb_ref, bdq_ref, bdk_ref, bdv_ref, wg_ref,
              bg_ref, xpad_ref, dst, *, ns, seg):
    e, dh = D_INNER, HEAD_DIM
    pad = SUBLANES
    rows = ns * seg
    tiles_per_head = dh // MXU_DIM
    xn = _rmsnorm(x_ref[...], ng_ref[...]).astype(BF16)

    def project(h):
        t0 = h * tiles_per_head
        xm_ = jnp.concatenate(
            [_dot(xn, win_ref[t0 + j]) for j in range(tiles_per_head)], axis=1)
        z_ = jnp.concatenate(
            [_dot(xn, win_ref[e // MXU_DIM + t0 + j]) for j in range(tiles_per_head)], axis=1)
        return xm_, z_

    projected = project(0)
    gates = bg_ref[...]
    yield
    for h in range(HEADS):
        c0, c1 = h * dh, (h + 1) * dh
        t0 = h * tiles_per_head
        xm, z = projected
        if h + 1 < HEADS:
            projected = project(h + 1)
        xpad_ref[:, pad:pad + seg, c0:c1] = xm.reshape(ns, seg, dh)
        pre = cb_ref[:, c0:c1] + cw_ref[CONV_W - 1:CONV_W, c0:c1] * xm
        for d in range(1, CONV_W):
            shifted = xpad_ref[:, pad - d:pad - d + seg, c0:c1].reshape(rows, dh)
            pre = pre + cw_ref[CONV_W - 1 - d:CONV_W - d, c0:c1] * shifted
        xc = jax.nn.silu(pre)
        xcb = xc.astype(BF16)
        xmb = xm.astype(BF16)

        def blockdiag(a, w_ref):
            return jnp.concatenate(
                [_dot(a[:, j * MXU_DIM:(j + 1) * MXU_DIM], w_ref[t0 + j])
                 for j in range(tiles_per_head)], axis=1)

        q = blockdiag(xcb, bdq_ref)
        k = blockdiag(xcb, bdk_ref)
        v = blockdiag(xmb, bdv_ref)
        qb, vb = q.astype(BF16), v.astype(BF16)
        gates = gates + _dot(xcb, wg_ref[0, c0:c1, :]) + _dot(xmb, wg_ref[1, c0:c1, :])
        dst['xc'][:, c0:c1] = xc
        dst['sz'][:, c0:c1] = jax.nn.silu(z)
        dst['qb'][:, c0:c1] = qb
        dst['ks'][:, c0:c1] = k * (dh ** -0.5)
        dst['vb'][:, c0:c1] = vb
        if h == HEADS - 1:
            dst['gates'][...] = gates
        yield


def _backend(xres_ref, src, cin_ref, nin_ref, min_ref, hng_ref, skip_ref, wout_ref, fg_ref,
             y_ref, c_ref, n_ref, m_ref, *, ns, seg, heads=HEADS, head0=None, acc_ref=None,
             memory_read_first=False):
    dh = HEAD_DIM
    n_out = D_MODEL // MXU_DIM
    ti = lax.broadcasted_iota(jnp.int32, (seg, seg), 0)
    si = lax.broadcasted_iota(jnp.int32, (seg, seg), 1)
    causal = si <= ti
    pairs = [(s, h) for s in range(ns) for h in range(heads)]

    def block(name, s, h):
        return src[name][s * seg:(s + 1) * seg, h * dh:(h + 1) * dh]

    def local(x):
        if head0 is None:
            return x
        return pltpu.roll(x, (LANES - head0) % LANES, 1)

    qc, cprev = {}, {}
    if memory_read_first:
        qc = {(s, h): _dot(block('qb', s, h), cin_ref[s, h].astype(BF16)) for s, h in pairs}
    gt = []
    for s in range(ns):
        g = src['gates'][s * seg:(s + 1) * seg, :]
        ig = g[:, 0:LANES]
        lf = _log_sigmoid(g[:, LANES:2 * LANES])
        b = _prefix_rows(lf, jnp.add, 0.0)
        a = ig - b
        cm = _prefix_rows(a, jnp.maximum, -jnp.inf)
        m_prev = min_ref[s]
        neg_mt_plus_b = -jnp.maximum(m_prev, cm)
        b_last = b[seg - 1:seg, :]
        m_new = b_last + jnp.maximum(m_prev, cm[seg - 1:seg, :])
        decay = jnp.broadcast_to(jnp.exp(b_last + m_prev - m_new), (SUBLANES, LANES))
        gt.append(dict(
            w_inter=local(jnp.exp(m_prev + neg_mt_plus_b)),
            inv_floor=local(jnp.exp(neg_mt_plus_b - b)),
            w=local(jnp.exp(b_last + a - m_new)),
            decay=local(decay)[0:1, :],
            a_rows=_cols_to_rows(local(a), seg, heads) * LOG2E,
            u=local(neg_mt_plus_b) * LOG2E,
            n_prev=nin_ref[s]))
        m_ref[s] = m_new
    yield

    sc = {}
    for s, h in pairs:
        w_intra = jnp.where(
            causal, jnp.exp2(gt[s]['u'][:, h:h + 1] + gt[s]['a_rows'][h:h + 1, :]), 0.0)
        sc[s, h] = lax.dot_general(block('qb', s, h), block('ks', s, h).astype(BF16),
                                   (((1,), (1,)), ((), ())),
                                   preferred_element_type=F32) * w_intra
        if not memory_read_first:
            cprev[s, h] = cin_ref[s, h]
            qc[s, h] = _dot(block('qb', s, h), cprev[s, h].astype(BF16))
    yield

    num = {(s, h): _dot(sc[s, h].astype(BF16), block('vb', s, h))
           + gt[s]['w_inter'][:, h:h + 1] * qc[s, h] for s, h in pairs}
    yield

    out, n_new = {}, {}
    for s, h in pairs:
        c0, c1 = h * dh, (h + 1) * dh
        n_prev = gt[s]['n_prev'][h:h + 1, :]
        qn = jnp.sum(block('qb', s, h).astype(F32) * n_prev, axis=1, keepdims=True)
        den = jnp.sum(sc[s, h], axis=1, keepdims=True) + gt[s]['w_inter'][:, h:h + 1] * qn
        r = 1.0 / jnp.maximum(jnp.abs(den), gt[s]['inv_floor'][:, h:h + 1])
        cen = num[s, h] - jnp.mean(num[s, h], axis=1, keepdims=True)
        var = jnp.mean(cen * cen, axis=1, keepdims=True)
        hn = cen * (r * lax.rsqrt(r * r * var + LN_EPS)) * hng_ref[:, c0:c1]
        out[s, h] = ((hn + skip_ref[:, c0:c1] * block('xc', s, h))
                     * block('sz', s, h)).astype(BF16)

        decay = gt[s]['decay'][:, h:h + 1]
        kw = block('ks', s, h) * gt[s]['w'][:, h:h + 1]
        upd = lax.dot_general(kw.astype(BF16), block('vb', s, h), (((0,), (0,)), ((), ())),
                              preferred_element_type=F32)
        c_old = cin_ref[s, h] if memory_read_first else cprev[s, h]
        c_ref[s, h] = decay * c_old + upd
        n_new[s, h] = decay * n_prev + jnp.sum(kw, axis=0, keepdims=True)
    for s in range(ns):
        n_ref[s] = jnp.concatenate([n_new[s, h] for h in range(heads)], axis=0)
    yield

    y_rows = []
    for s in range(ns):
        acc = [None] * n_out
        for h in range(heads):
            for n in range(n_out):
                part = _dot(out[s, h], wout_ref[n, h * dh:(h + 1) * dh, :])
                acc[n] = part if acc[n] is None else acc[n] + part
        y_rows.append(jnp.concatenate(acc, axis=1))
    proj = y_rows[0] if ns == 1 else jnp.concatenate(y_rows, axis=0)
    if acc_ref is not None:
        proj = jnp.where(head0 == 0, 0.0, acc_ref[...]) + proj
        acc_ref[...] = proj
    y_ref[...] = _rmsnorm(xres_ref[...] + proj, fg_ref[...])


def _mlstm_pipelined_kernel(x_ref, xprev_ref, ng_ref, fg_ref, win_ref, cw_ref, cb_ref,
                            bd_ref, wg_ref, bg_ref, hng_ref, skip_ref,
                            wout_ref, y_ref, c_ref, n_ref, m_ref, conv_ref, xpad_ref,
                            *handoff, seg, n_tiles):
    t = pl.program_id(0)
    e = D_INNER
    pad = SUBLANES
    n_hand = len(_HANDOFF)
    sets = (dict(zip(_HANDOFF_NAMES, handoff[:n_hand])),
            dict(zip(_HANDOFF_NAMES, handoff[n_hand:])))

    @pl.when(t == 0)
    def _():
        for ref in sets[1].values():
            ref[...] = jnp.zeros(ref.shape, ref.dtype)

    @pl.when(t % n_tiles == 0)
    def _():
        xpad_ref[:, 0:pad, :] = jnp.zeros((1, pad, e), F32)

    @pl.when((t == 0) | ((t - 1) % n_tiles == 0))
    def _():
        c_ref[...] = jnp.zeros(c_ref.shape, F32)
        n_ref[...] = jnp.zeros(n_ref.shape, F32)
        m_ref[...] = jnp.zeros(m_ref.shape, F32)

    def step(dst, src):
        front = _frontend(x_ref, ng_ref, win_ref, cw_ref, cb_ref, bd_ref.at[0], bd_ref.at[1],
                          bd_ref.at[2], wg_ref, bg_ref, xpad_ref, dst, ns=1, seg=seg)
        back = _backend(xprev_ref, src, c_ref, n_ref, m_ref, hng_ref, skip_ref, wout_ref,
                        fg_ref, y_ref, c_ref, n_ref, m_ref, ns=1, seg=seg)
        _interleave(back, front)
        conv_ref[...] = xpad_ref[:, pad + seg - (CONV_W - 1):pad + seg, :]
        xpad_ref[:, 0:pad, :] = xpad_ref[:, seg:seg + pad, :]

    @pl.when(t % 2 == 0)
    def _():
        step(sets[0], sets[1])

    @pl.when(t % 2 == 1)
    def _():
        step(sets[1], sets[0])


def _mlstm_front_kernel(x_ref, ng_ref, win_ref, cw_ref, cb_ref, bd_ref, wg_ref, bg_ref,
                        conv0_ref, *rest, ns, seg):
    n_hand = len(_HANDOFF)
    dst = dict(zip(_HANDOFF_NAMES, rest[:n_hand]))
    conv_ref, xpad_ref = rest[n_hand:]
    pad = SUBLANES
    xpad_ref[:, pad - (CONV_W - 1):pad, :] = conv0_ref[...]
    _interleave(_frontend(x_ref, ng_ref, win_ref, cw_ref, cb_ref, bd_ref.at[0], bd_ref.at[1],
                          bd_ref.at[2], wg_ref, bg_ref, xpad_ref, dst, ns=ns, seg=seg))
    conv_ref[...] = xpad_ref[:, pad + seg - (CONV_W - 1):pad + seg, :]


HEAD_GROUP = 2


def _sgu_recurrence_kernel(*refs, tm, seg):
    n_hand = len(_HANDOFF)
    sgu_refs, rest = refs[:8], refs[8:]
    xres_ref = rest[0]
    src = dict(zip(_HANDOFF_NAMES, rest[1:1 + n_hand]))
    (c0_ref, n0_ref, m0_ref, hng_ref, skip_ref, wout_ref, fg_ref,
     y_ref, ys_ref, c_ref, n_ref, m_ref, acc_ref) = rest[1 + n_hand:]
    i = pl.program_id(0)
    head0 = (i % (HEADS // HEAD_GROUP)) * HEAD_GROUP

    @pl.when(i == 0)
    def _():
        acc_ref[...] = jnp.zeros(acc_ref.shape, F32)

    _interleave(
        _backend(xres_ref, src, c0_ref, n0_ref.at[0], m0_ref, hng_ref, skip_ref, wout_ref,
                 fg_ref, ys_ref, c_ref, n_ref.at[0], m_ref, ns=1, seg=seg,
                 heads=HEAD_GROUP, head0=head0, acc_ref=acc_ref, memory_read_first=True),
        _sgu_stages(*sgu_refs, y_ref, (), tm=tm))


def _state_specs(ns, index):
    dh = HEAD_DIM
    return [pl.BlockSpec((ns, HEADS, dh, dh), lambda *i: (index(*i), 0, 0, 0)),
            pl.BlockSpec((ns, HEADS, dh), lambda *i: (index(*i), 0, 0)),
            pl.BlockSpec((ns, 1, LANES), lambda *i: (index(*i), 0, 0))]


def _state_shapes(n_streams):
    dh = HEAD_DIM
    return [jax.ShapeDtypeStruct((n_streams, HEADS, dh, dh), F32),
            jax.ShapeDtypeStruct((n_streams, HEADS, dh), F32),
            jax.ShapeDtypeStruct((n_streams, 1, LANES), F32)]


def _front_weight_args(w):
    return [w['norm_g'], w['w_in'], w['conv_w'], w['conv_b'], w['bd'], w['wg'], w['bg']]


def _front_weight_specs(w):
    return [_const_spec(a.shape) for a in _front_weight_args(w)]


def _mlstm_prompt(x, w, *, n_streams, seg):
    e = D_INNER
    n_tiles = x.shape[0] // (n_streams * seg)
    total = n_streams * n_tiles

    def front_tile(t):
        return jnp.minimum(t, total - 1)

    def back_tile(t):
        return jnp.maximum(t - 1, 0)

    cur_spec = pl.BlockSpec((seg, D_MODEL), lambda t: (front_tile(t), 0))
    prev_spec = pl.BlockSpec((seg, D_MODEL), lambda t: (back_tile(t), 0))
    conv_spec = pl.BlockSpec((1, CONV_W - 1, e), lambda t: (front_tile(t) // n_tiles, 0, 0))
    back_specs = [_const_spec((1, e)), _const_spec((1, e)), _const_spec(w['w_out'].shape)]
    fw = _front_weight_specs(w)
    in_specs = [cur_spec, prev_spec, fw[0], _const_spec((1, D_MODEL))] + fw[1:] + back_specs
    fa = _front_weight_args(w)
    args = [x, x, fa[0], w['final_g']] + fa[1:] + [w['hn_g'], w['skip'], w['w_out']]
    handoff = [pltpu.VMEM((seg, width), dtype)
               for _ in range(2) for _, width, dtype in _HANDOFF]
    return pl.pallas_call(
        functools.partial(_mlstm_pipelined_kernel, seg=seg, n_tiles=n_tiles),
        grid=(total + 1,),
        in_specs=in_specs,
        out_specs=([prev_spec] + _state_specs(1, lambda t: back_tile(t) // n_tiles)
                   + [conv_spec]),
        out_shape=([jax.ShapeDtypeStruct(x.shape, F32)] + _state_shapes(n_streams)
                   + [jax.ShapeDtypeStruct((n_streams, CONV_W - 1, e), F32)]),
        scratch_shapes=[pltpu.VMEM((1, SUBLANES + seg, e), F32)] + handoff,
        compiler_params=pltpu.CompilerParams(
            dimension_semantics=("arbitrary",),
            vmem_limit_bytes=VMEM_LIMIT_BYTES),
        name="mlstm_prompt",
    )(*args)


def _mlstm_sample_front(x, w, conv0, *, seg, ns_front):
    e = D_INNER
    n_streams = conv0.shape[0]
    rows_f = ns_front * seg
    hand_shapes = [jax.ShapeDtypeStruct((x.shape[0], width), dtype)
                   for _, width, dtype in _HANDOFF]
    conv_spec = pl.BlockSpec((ns_front, CONV_W - 1, e), lambda i: (i, 0, 0))
    res = pl.pallas_call(
        functools.partial(_mlstm_front_kernel, ns=ns_front, seg=seg),
        grid=(n_streams // ns_front,),
        in_specs=([pl.BlockSpec((rows_f, D_MODEL), lambda i: (i, 0))]
                  + _front_weight_specs(w) + [conv_spec]),
        out_specs=([pl.BlockSpec((rows_f, width), lambda i: (i, 0)) for _, width, _ in _HANDOFF]
                   + [conv_spec]),
        out_shape=hand_shapes + [jax.ShapeDtypeStruct((n_streams, CONV_W - 1, e), F32)],
        scratch_shapes=[pltpu.VMEM((ns_front, SUBLANES + seg, e), F32)],
        compiler_params=pltpu.CompilerParams(
            dimension_semantics=("arbitrary",), vmem_limit_bytes=VMEM_LIMIT_BYTES),
        name="mlstm_sample_front",
    )(x, *_front_weight_args(w), conv0)
    return res[:-1], res[-1]


def _sgu_prompt_with_sample_recurrence(xp, sgu_args, xs, hand, state, w, *, seg):
    e, dh = D_INNER, HEAD_DIM
    c0, n0, m0 = state
    n_streams = c0.shape[0]
    groups = HEADS // HEAD_GROUP
    tm = MXU_DIM
    steps = xp.shape[0] // tm
    assert steps == n_streams * groups, (steps, n_streams, groups)
    gw = HEAD_GROUP * dh
    norm_g, w_in, ln_g, ln_b, w_s, b_s, w_out = sgu_args

    def stream(i):
        return i // groups

    def group(i):
        return i % groups

    sgu_specs = [pl.BlockSpec((tm, D_MODEL), lambda i: (i, 0)),
                 _const_spec((1, D_MODEL)), _const_spec(w_in.shape),
                 _const_spec((1, e)), _const_spec((1, e)),
                 _const_spec((SGU_GROUPS, SGU_CHUNK, SGU_CHUNK)),
                 _const_spec((SGU_CHUNK, SGU_GROUPS)), _const_spec(w_out.shape)]
    hand_specs = [pl.BlockSpec((seg, gw), lambda i: (stream(i), group(i))) if width == e
                  else pl.BlockSpec((seg, width), lambda i: (stream(i), 0))
                  for _, width, _ in _HANDOFF]
    c_spec = pl.BlockSpec((1, HEAD_GROUP, dh, dh), lambda i: (stream(i), group(i), 0, 0))
    n_spec = pl.BlockSpec((1, 1, HEAD_GROUP, dh), lambda i: (stream(i), group(i), 0, 0))
    m_spec = pl.BlockSpec((1, 1, LANES), lambda i: (stream(i), 0, 0))
    ys_spec = pl.BlockSpec((seg, D_MODEL), lambda i: (stream(i), 0))
    head_vec_spec = pl.BlockSpec((1, gw), lambda i: (0, group(i)))
    wout_spec = pl.BlockSpec((D_MODEL // MXU_DIM, gw, MXU_DIM), lambda i: (0, group(i), 0))
    n0g = n0.reshape(n_streams, groups, HEAD_GROUP, dh)
    xp1, ys, c_out, n_out, m_out = pl.pallas_call(
        functools.partial(_sgu_recurrence_kernel, tm=tm, seg=seg),
        grid=(steps,),
        in_specs=(sgu_specs + [ys_spec] + hand_specs + [c_spec, n_spec, m_spec]
                  + [head_vec_spec, head_vec_spec, wout_spec, _const_spec((1, D_MODEL))]),
        out_specs=[pl.BlockSpec((tm, D_MODEL), lambda i: (i, 0)), ys_spec,
                   c_spec, n_spec, m_spec],
        out_shape=[jax.ShapeDtypeStruct(xp.shape, F32), jax.ShapeDtypeStruct(xs.shape, F32),
                   jax.ShapeDtypeStruct(c0.shape, F32), jax.ShapeDtypeStruct(n0g.shape, F32),
                   jax.ShapeDtypeStruct(m0.shape, F32)],
        scratch_shapes=[pltpu.VMEM((seg, D_MODEL), F32)],
        compiler_params=pltpu.CompilerParams(
            dimension_semantics=("arbitrary",), vmem_limit_bytes=VMEM_LIMIT_BYTES),
        name="sgu_prompt_sample_recurrence",
    )(xp, norm_g, w_in, ln_g, ln_b, w_s, b_s, w_out, xs, *hand, c0, n0g, m0,
      w['hn_g'], w['skip'], w['w_out'], w['final_g'])
    return xp1, ys, c_out, n_out.reshape(n0.shape), m_out


def _qkv_weights_kernel(rows_ref, wg_ref, bd_ref, fold_ref):
    nt = rows_ref.shape[1]
    r_blk = lax.broadcasted_iota(jnp.int32, (MXU_DIM, MXU_DIM), 0) // QKV_BLOCK
    c_idx = lax.broadcasted_iota(jnp.int32, (MXU_DIM, MXU_DIM), 1)
    same_block = r_blk == c_idx // QKV_BLOCK
    for j in range(nt):
        r0, r1 = j * MXU_DIM, (j + 1) * MXU_DIM
        tiles = []
        for a in range(3):
            rows = rows_ref[a, j]
            tiled = jnp.zeros((MXU_DIM, MXU_DIM), F32)
            for o in range(QKV_BLOCK):
                tiled = jnp.where(c_idx % QKV_BLOCK == o, rows[:, o:o + 1], tiled)
            tile = jnp.where(same_block, tiled, 0.0).astype(BF16)
            bd_ref[a, j] = tile
            tiles.append(tile)
        fold_ref[0, r0:r1, :] = (_dot(tiles[0], wg_ref[0, r0:r1, :])
                                 + _dot(tiles[1], wg_ref[1, r0:r1, :])).astype(BF16)
        fold_ref[1, r0:r1, :] = _dot(tiles[2], wg_ref[2, r0:r1, :]).astype(BF16)


def _qkv_weights(wq, wk, wv, wg):
    nt = D_INNER // MXU_DIM
    rows = jnp.stack([wq, wk, wv]).reshape(3, nt, MXU_DIM, QKV_BLOCK)
    return pl.pallas_call(
        _qkv_weights_kernel,
        out_shape=[jax.ShapeDtypeStruct((3, nt, MXU_DIM, MXU_DIM), BF16),
                   jax.ShapeDtypeStruct((2, D_INNER, wg.shape[-1]), BF16)],
        name="qkv_weights",
    )(rows, wg)


def _head_lanes(w):
    lead = w.shape[:-1]
    z = jnp.zeros(lead + (LANES - HEADS,), w.dtype)
    return jnp.concatenate([w[..., :HEADS], z, w[..., HEADS:], z], axis=-1)


def kernel(x_prompt, x_sample, state_mlstm_C, state_mlstm_n, state_mlstm_m, state_mlstm_conv, norm_g, final_norm_g, a_w_in, a_ln_g, a_ln_b, a_w_s, a_b_s, a_w_out, b_w_in, b_conv_w, b_conv_b, b_wq, b_wk, b_wv, b_w_gates, b_b_gates, b_hnorm_g, b_skip, b_w_out):
    bsz, seq, d = x_prompt.shape
    dec_b, dec_t, _ = x_sample.shape
    e = D_INNER
    assert norm_g.shape[0] == 2 and a_w_in.shape[0] == 1 and b_w_in.shape[0] == 1
    assert d == D_MODEL and seq % MXU_DIM == 0 and SGU_CHUNK % dec_t == 0

    xp = x_prompt.reshape(bsz * seq, d)
    xs = x_sample.reshape(dec_b * dec_t, d)

    a_in = _col_slabs(a_w_in[0])
    a_out = _col_slabs(a_w_out[0])
    ng0 = norm_g[0].reshape(1, d)
    lng = a_ln_g[0].reshape(1, e)
    lnb = a_ln_b[0].reshape(1, e)
    per_tile = SGU_CHUNK // dec_t
    w_head = a_w_s[0][:, :dec_t, :dec_t]
    ws_s = jnp.einsum('ab,gts->gatbs', np.eye(per_tile, dtype=np.float32), w_head)
    ws_s = ws_s.reshape(SGU_GROUPS, SGU_CHUNK, SGU_CHUNK)
    bs_s = jnp.tile(a_b_s[0][:, :dec_t], (1, per_tile)).T
    xs1, v_s = _sgu_layer(xs, ng0, a_in, lng, lnb, ws_s, bs_s, a_out,
                          tm=2 * SGU_CHUNK, want_v=True)

    wg = _head_lanes(b_w_gates[0]).astype(BF16).reshape(3, e, 2 * LANES)
    bd, wg_folded = _qkv_weights(b_wq[0], b_wk[0], b_wv[0], wg)
    w = dict(norm_g=norm_g[1].reshape(1, d), final_g=final_norm_g.reshape(1, d),
             w_in=_col_slabs(b_w_in[0]), conv_w=b_conv_w[0], conv_b=b_conv_b[0].reshape(1, e),
             bd=bd, wg=wg_folded,
             bg=_head_lanes(b_b_gates[0]).reshape(1, 2 * LANES),
             hn_g=b_hnorm_g[0].reshape(1, e), skip=b_skip[0].reshape(1, e),
             w_out=_col_slabs(b_w_out[0]))
    hand, conv_s = _mlstm_sample_front(xs1, w, state_mlstm_conv[0], seg=dec_t,
                                       ns_front=dec_b // 2)
    m0 = jnp.pad(state_mlstm_m[0], ((0, 0), (0, LANES - HEADS))).reshape(dec_b, 1, LANES)
    xp1, ys, c_s, n_s, m_s = _sgu_prompt_with_sample_recurrence(
        xp, (ng0, a_in, lng, lnb, a_w_s[0], a_b_s[0].T, a_out), xs1, hand,
        (state_mlstm_C[0], state_mlstm_n[0], m0), w, seg=dec_t)
    yp, c_p, n_p, m_p, conv_p = _mlstm_prompt(xp1, w, n_streams=bsz, seg=MXU_DIM)

    return (yp.reshape(bsz, seq, d),
            ys.reshape(dec_b, dec_t, d),
            v_s.reshape(1, dec_b, dec_t, e),
            c_p[None], n_p[None], m_p[:, 0, :HEADS][None], conv_p[None],
            c_s[None], n_s[None], m_s[:, 0, :HEADS][None], conv_s[None])
```

```python
import functools
import math

import numpy as np
import jax
import jax.numpy as jnp
from jax import lax
from jax.experimental import pallas as pl
from jax.experimental.pallas import tpu as pltpu

D_MODEL = 1024
D_INNER = 2048
SGU_CHUNK = 128
SGU_GROUPS = 8
SGU_GROUP_DIM = D_INNER // SGU_GROUPS
HEADS = 4
HEAD_DIM = D_INNER // HEADS
QKV_BLOCK = 4
CONV_W = 4
RMS_EPS = 1e-6
LN_EPS = 1e-5

LANES = 128
SUBLANES = 8
MXU_DIM = 256
VMEM_LIMIT_BYTES = 60 * 1024 * 1024
SLAB_BLOCK_BYTES = 6 * 1024 * 1024

BF16 = jnp.bfloat16
F32 = jnp.float32
LOG2E = math.log2(math.e)

assert SGU_GROUP_DIM == MXU_DIM


def _dot(a, b):
    return jnp.dot(a, b, preferred_element_type=F32)


def _rmsnorm(x, g):
    return x * lax.rsqrt(jnp.mean(x * x, axis=-1, keepdims=True) + RMS_EPS) * g


def _log_sigmoid(x):
    return jnp.minimum(x, 0.0) - jnp.log1p(jnp.exp(-jnp.abs(x)))


def _prefix_rows(x, op, fill):
    n = x.shape[0]
    row = lax.broadcasted_iota(jnp.int32, x.shape, 0)
    shift = 1
    while shift < n:
        if shift % SUBLANES == 0:
            shifted = jnp.concatenate(
                [jnp.full((shift, x.shape[1]), fill, x.dtype), x[:n - shift]], axis=0)
        else:
            shifted = jnp.where(row >= shift, pltpu.roll(x, shift, 0), fill)
        x = op(x, shifted)
        shift *= 2
    return x


def _const_spec(shape):
    zeros = (0,) * len(shape)
    return pl.BlockSpec(shape, lambda *_: zeros, pipeline_mode=pl.Buffered(1))


def _slab_kernel(w_ref, o_ref):
    for j in range(o_ref.shape[0]):
        o_ref[j] = w_ref[:, j * MXU_DIM:(j + 1) * MXU_DIM].astype(BF16)


def _col_slabs(w):
    k, n = w.shape
    rows = MXU_DIM
    while 2 * rows <= k and 2 * rows * n * 4 <= SLAB_BLOCK_BYTES:
        rows *= 2
    return pl.pallas_call(
        _slab_kernel,
        grid=(k // rows,),
        in_specs=[pl.BlockSpec((rows, n), lambda i: (i, 0))],
        out_specs=pl.BlockSpec((n // MXU_DIM, rows, MXU_DIM), lambda i: (0, i, 0)),
        out_shape=jax.ShapeDtypeStruct((n // MXU_DIM, k, MXU_DIM), BF16),
        compiler_params=pltpu.CompilerParams(dimension_semantics=("arbitrary",)),
        name="weight_slabs",
    )(w)


def _sgu_kernel(x_ref, ng_ref, win_ref, lng_ref, lnb_ref, ws_ref, bs_ref, wout_ref,
                y_ref, *v_refs, tm):
    _interleave(_sgu_stages(x_ref, ng_ref, win_ref, lng_ref, lnb_ref, ws_ref, bs_ref, wout_ref,
                            y_ref, v_refs, tm=tm))


def _sgu_stages(x_ref, ng_ref, win_ref, lng_ref, lnb_ref, ws_ref, bs_ref, wout_ref,
                y_ref, v_refs, *, tm):
    ng = SGU_GROUPS
    e = D_INNER
    x = x_ref[...]
    xn = _rmsnorm(x, ng_ref[...]).astype(BF16)

    v_g, uz_g = [], []
    row_sum = jnp.zeros((tm, 1), F32)
    for g in range(ng):
        v = jax.nn.gelu(_dot(xn, win_ref[ng + g]))
        row_sum = row_sum + jnp.sum(v, axis=1, keepdims=True)
        v_g.append(v)
        if g % 4 == 3:
            yield
    mu = row_sum * (1.0 / e)
    sq_sum = jnp.zeros((tm, 1), F32)
    for g in range(ng):
        u = jax.nn.gelu(_dot(xn, win_ref[g]))
        z = _dot(xn, win_ref[2 * ng + g])
        uz_g.append(u * jax.nn.silu(z))
        v_g[g] = v_g[g] - mu
        sq_sum = sq_sum + jnp.sum(v_g[g] * v_g[g], axis=1, keepdims=True)
        if g % 4 == 3:
            yield
    rstd = lax.rsqrt(sq_sum * (1.0 / e) + LN_EPS)

    ti = lax.broadcasted_iota(jnp.int32, (SGU_CHUNK, SGU_CHUNK), 0)
    si = lax.broadcasted_iota(jnp.int32, (SGU_CHUNK, SGU_CHUNK), 1)
    causal = si <= ti
    s_g = []
    for g in range(ng):
        lo, hi = g * SGU_GROUP_DIM, (g + 1) * SGU_GROUP_DIM
        vn = v_g[g] * rstd * lng_ref[:, lo:hi] + lnb_ref[:, lo:hi]
        if v_refs:
            v_refs[0][:, lo:hi] = vn
        vb = vn.astype(BF16)
        wm = jnp.where(causal, ws_ref[g], 0.0).astype(BF16)
        bias = bs_ref[:, g:g + 1]
        s_g.append(jnp.concatenate(
            [_dot(wm, vb[c * SGU_CHUNK:(c + 1) * SGU_CHUNK]) + bias
             for c in range(tm // SGU_CHUNK)], axis=0))
    yield
    n_out = D_MODEL // MXU_DIM
    acc = [None] * n_out
    out_next = (uz_g[0] * s_g[0]).astype(BF16)
    for g in range(ng):
        lo, hi = g * SGU_GROUP_DIM, (g + 1) * SGU_GROUP_DIM
        out = out_next
        if g + 1 < ng:
            out_next = (uz_g[g + 1] * s_g[g + 1]).astype(BF16)
        for n in range(n_out):
            part = _dot(out, wout_ref[n, lo:hi, :])
            acc[n] = part if acc[n] is None else acc[n] + part
        if g == ng // 2 - 1:
            yield
    y_ref[...] = x + jnp.concatenate(acc, axis=1)


def _sgu_layer(x, norm_g, w_in, ln_g, ln_b, w_s, b_s, w_out, *, tm, want_v):
    rows = x.shape[0]
    e = D_INNER
    row_spec = pl.BlockSpec((tm, D_MODEL), lambda i: (i, 0))
    out_shape = [jax.ShapeDtypeStruct((rows, D_MODEL), F32)]
    out_specs = [row_spec]
    if want_v:
        out_shape.append(jax.ShapeDtypeStruct((rows, e), F32))
        out_specs.append(pl.BlockSpec((tm, e), lambda i: (i, 0)))
    res = pl.pallas_call(
        functools.partial(_sgu_kernel, tm=tm),
        grid=(rows // tm,),
        in_specs=[row_spec,
                  _const_spec((1, D_MODEL)),
                  _const_spec(w_in.shape),
                  _const_spec((1, e)),
                  _const_spec((1, e)),
                  _const_spec((SGU_GROUPS, SGU_CHUNK, SGU_CHUNK)),
                  _const_spec((SGU_CHUNK, SGU_GROUPS)),
                  _const_spec(w_out.shape)],
        out_specs=out_specs,
        out_shape=out_shape,
        compiler_params=pltpu.CompilerParams(
            dimension_semantics=("arbitrary",), vmem_limit_bytes=VMEM_LIMIT_BYTES),
        name="sgu_layer_v" if want_v else "sgu_layer",
    )(x, norm_g, w_in, ln_g, ln_b, w_s, b_s, w_out)
    return res


def _cols_to_rows(a, seg, heads):
    if seg % LANES == 0:
        return a.T[0:SUBLANES, :]
    ti = lax.broadcasted_iota(jnp.int32, (seg, seg), 0)
    si = lax.broadcasted_iota(jnp.int32, (seg, seg), 1)
    return jnp.concatenate(
        [jnp.sum(jnp.where(ti == si, a[:, h:h + 1], 0.0), axis=0, keepdims=True)
         for h in range(heads)], axis=0)


_HANDOFF = (('xc', D_INNER, F32), ('sz', D_INNER, F32), ('qb', D_INNER, BF16),
            ('ks', D_INNER, F32), ('vb', D_INNER, BF16), ('gates', 2 * LANES, F32))
_HANDOFF_NAMES = tuple(name for name, _, _ in _HANDOFF)


def _interleave(*gens):
    live = list(gens)
    while live:
        for gen in list(live):
            try:
                next(gen)
            except StopIteration:
                live.remove(gen)


def _frontend(x_ref, ng_ref, win_ref, cw_ref, cb_ref, bdq_ref, bdk_ref, bdv_ref, wg_ref,
              bg_ref, xpad_ref, dst, *, ns, seg):
    e, dh = D_INNER, HEAD_DIM
    pad = SUBLANES
    rows = ns * seg
    tiles_per_head = dh // MXU_DIM
    xn = _rmsnorm(x_ref[...], ng_ref[...]).astype(BF16)

    def project(h):
        t0 = h * tiles_per_head
        xm_ = jnp.concatenate(
            [_dot(xn, win_ref[t0 + j]) for j in range(tiles_per_head)], axis=1)
        z_ = jnp.concatenate(
            [_dot(xn, win_ref[e // MXU_DIM + t0 + j]) for j in range(tiles_per_head)], axis=1)
        return xm_, z_

    projected = project(0)
    gates = bg_ref[...]
    yield
    for h in range(HEADS):
        c0, c1 = h * dh, (h + 1) * dh
        t0 = h * tiles_per_head
        xm, z = projected
        if h + 1 < HEADS:
            projected = project(h + 1)
        xpad_ref[:, pad:pad + seg, c0:c1] = xm.reshape(ns, seg, dh)
        pre = cb_ref[:, c0:c1] + cw_ref[CONV_W - 1:CONV_W, c0:c1] * xm
        for d in range(1, CONV_W):
            shifted = xpad_ref[:, pad - d:pad - d + seg, c0:c1].reshape(rows, dh)
            pre = pre + cw_ref[CONV_W - 1 - d:CONV_W - d, c0:c1] * shifted
        xc = jax.nn.silu(pre)
        xcb = xc.astype(BF16)
        xmb = xm.astype(BF16)

        def blockdiag(a, w_ref):
            return jnp.concatenate(
                [_dot(a[:, j * MXU_DIM:(j + 1) * MXU_DIM], w_ref[t0 + j])
                 for j in range(tiles_per_head)], axis=1)

        q = blockdiag(xcb, bdq_ref)
        k = blockdiag(xcb, bdk_ref)
        v = blockdiag(xmb, bdv_ref)
        qb, vb = q.astype(BF16), v.astype(BF16)
        gates = gates + _dot(xcb, wg_ref[0, c0:c1, :]) + _dot(xmb, wg_ref[1, c0:c1, :])
        dst['xc'][:, c0:c1] = xc
        dst['sz'][:, c0:c1] = jax.nn.silu(z)
        dst['qb'][:, c0:c1] = qb
        dst['ks'][:, c0:c1] = k * (dh ** -0.5)
        dst['vb'][:, c0:c1] = vb
        if h == HEADS - 1:
            dst['gates'][...] = gates
        yield


def _backend(xres_ref, src, cin_ref, nin_ref, min_ref, hng_ref, skip_ref, wout_ref, fg_ref,
             y_ref, c_ref, n_ref, m_ref, *, ns, seg, heads=HEADS, head0=None, acc_ref=None,
             memory_read_first=False):
    dh = HEAD_DIM
    n_out = D_MODEL // MXU_DIM
    ti = lax.broadcasted_iota(jnp.int32, (seg, seg), 0)
    si = lax.broadcasted_iota(jnp.int32, (seg, seg), 1)
    causal = si <= ti
    pairs = [(s, h) for s in range(ns) for h in range(heads)]

    def block(name, s, h):
        return src[name][s * seg:(s + 1) * seg, h * dh:(h + 1) * dh]

    def local(x):
        if head0 is None:
            return x
        return pltpu.roll(x, (LANES - head0) % LANES, 1)

    qc, cprev = {}, {}
    if memory_read_first:
        qc = {(s, h): _dot(block('qb', s, h), cin_ref[s, h].astype(BF16)) for s, h in pairs}
    gt = []
    for s in range(ns):
        g = src['gates'][s * seg:(s + 1) * seg, :]
        ig = g[:, 0:LANES]
        lf = _log_sigmoid(g[:, LANES:2 * LANES])
        b = _prefix_rows(lf, jnp.add, 0.0)
        a = ig - b
        cm = _prefix_rows(a, jnp.maximum, -jnp.inf)
        m_prev = min_ref[s]
        neg_mt_plus_b = -jnp.maximum(m_prev, cm)
        b_last = b[seg - 1:seg, :]
        m_new = b_last + jnp.maximum(m_prev, cm[seg - 1:seg, :])
        decay = jnp.broadcast_to(jnp.exp(b_last + m_prev - m_new), (SUBLANES, LANES))
        gt.append(dict(
            w_inter=local(jnp.exp(m_prev + neg_mt_plus_b)),
            inv_floor=local(jnp.exp(neg_mt_plus_b - b)),
            w=local(jnp.exp(b_last + a - m_new)),
            decay=local(decay)[0:1, :],
            a_rows=_cols_to_rows(local(a), seg, heads) * LOG2E,
            u=local(neg_mt_plus_b) * LOG2E,
            n_prev=nin_ref[s]))
        m_ref[s] = m_new
    yield

    sc = {}
    for s, h in pairs:
        w_intra = jnp.where(
            causal, jnp.exp2(gt[s]['u'][:, h:h + 1] + gt[s]['a_rows'][h:h + 1, :]), 0.0)
        sc[s, h] = lax.dot_general(block('qb', s, h), block('ks', s, h).astype(BF16),
                                   (((1,), (1,)), ((), ())),
                                   preferred_element_type=F32) * w_intra
        if not memory_read_first:
            cprev[s, h] = cin_ref[s, h]
            qc[s, h] = _dot(block('qb', s, h), cprev[s, h].astype(BF16))
    yield

    num = {(s, h): _dot(sc[s, h].astype(BF16), block('vb', s, h))
           + gt[s]['w_inter'][:, h:h + 1] * qc[s, h] for s, h in pairs}
    yield

    out, n_new = {}, {}
    for s, h in pairs:
        c0, c1 = h * dh, (h + 1) * dh
        n_prev = gt[s]['n_prev'][h:h + 1, :]
        qn = jnp.sum(block('qb', s, h).astype(F32) * n_prev, axis=1, keepdims=True)
        den = jnp.sum(sc[s, h], axis=1, keepdims=True) + gt[s]['w_inter'][:, h:h + 1] * qn
        r = 1.0 / jnp.maximum(jnp.abs(den), gt[s]['inv_floor'][:, h:h + 1])
        cen = num[s, h] - jnp.mean(num[s, h], axis=1, keepdims=True)
        var = jnp.mean(cen * cen, axis=1, keepdims=True)
        hn = cen * (r * lax.rsqrt(r * r * var + LN_EPS)) * hng_ref[:, c0:c1]
        out[s, h] = ((hn + skip_ref[:, c0:c1] * block('xc', s, h))
                     * block('sz', s, h)).astype(BF16)

        decay = gt[s]['decay'][:, h:h + 1]
        kw = block('ks', s, h) * gt[s]['w'][:, h:h + 1]
        upd = lax.dot_general(kw.astype(BF16), block('vb', s, h), (((0,), (0,)), ((), ())),
                              preferred_element_type=F32)
        c_old = cin_ref[s, h] if memory_read_first else cprev[s, h]
        c_ref[s, h] = decay * c_old + upd
        n_new[s, h] = decay * n_prev + jnp.sum(kw, axis=0, keepdims=True)
    for s in range(ns):
        n_ref[s] = jnp.concatenate([n_new[s, h] for h in range(heads)], axis=0)
    yield

    y_rows = []
    for s in range(ns):
        acc = [None] * n_out
        for h in range(heads):
            for n in range(n_out):
                part = _dot(out[s, h], wout_ref[n, h * dh:(h + 1) * dh, :])
                acc[n] = part if acc[n] is None else acc[n] + part
        y_rows.append(jnp.concatenate(acc, axis=1))
    proj = y_rows[0] if ns == 1 else jnp.concatenate(y_rows, axis=0)
    if acc_ref is not None:
        proj = jnp.where(head0 == 0, 0.0, acc_ref[...]) + proj
        acc_ref[...] = proj
    y_ref[...] = _rmsnorm(xres_ref[...] + proj, fg_ref[...])


def _mlstm_pipelined_kernel(x_ref, xprev_ref, ng_ref, fg_ref, win_ref, cw_ref, cb_ref,
                            bd_ref, wg_ref, bg_ref, hng_ref, skip_ref,
                            wout_ref, y_ref, c_ref, n_ref, m_ref, conv_ref, xpad_ref,
                            *handoff, seg, n_tiles):
    t = pl.program_id(0)
    e = D_INNER
    pad = SUBLANES
    n_hand = len(_HANDOFF)
    sets = (dict(zip(_HANDOFF_NAMES, handoff[:n_hand])),
            dict(zip(_HANDOFF_NAMES, handoff[n_hand:])))

    @pl.when(t == 0)
    def _():
        for ref in sets[1].values():
            ref[...] = jnp.zeros(ref.shape, ref.dtype)

    @pl.when(t % n_tiles == 0)
    def _():
        xpad_ref[:, 0:pad, :] = jnp.zeros((1, pad, e), F32)

    @pl.when((t == 0) | ((t - 1) % n_tiles == 0))
    def _():
        c_ref[...] = jnp.zeros(c_ref.shape, F32)
        n_ref[...] = jnp.zeros(n_ref.shape, F32)
        m_ref[...] = jnp.zeros(m_ref.shape, F32)

    def step(dst, src):
        front = _frontend(x_ref, ng_ref, win_ref, cw_ref, cb_ref, bd_ref.at[0], bd_ref.at[1],
                          bd_ref.at[2], wg_ref, bg_ref, xpad_ref, dst, ns=1, seg=seg)
        back = _backend(xprev_ref, src, c_ref, n_ref, m_ref, hng_ref, skip_ref, wout_ref,
                        fg_ref, y_ref, c_ref, n_ref, m_ref, ns=1, seg=seg)
        _interleave(back, front)
        conv_ref[...] = xpad_ref[:, pad + seg - (CONV_W - 1):pad + seg, :]
        xpad_ref[:, 0:pad, :] = xpad_ref[:, seg:seg + pad, :]

    @pl.when(t % 2 == 0)
    def _():
        step(sets[0], sets[1])

    @pl.when(t % 2 == 1)
    def _():
        step(sets[1], sets[0])


def _mlstm_front_kernel(x_ref, ng_ref, win_ref, cw_ref, cb_ref, bd_ref, wg_ref, bg_ref,
                        conv0_ref, *rest, ns, seg):
    n_hand = len(_HANDOFF)
    dst = dict(zip(_HANDOFF_NAMES, rest[:n_hand]))
    conv_ref, xpad_ref = rest[n_hand:]
    pad = SUBLANES
    for j in range(CONV_W - 1):
        xpad_ref[:, pad - (CONV_W - 1) + j, :] = conv0_ref[j]
    _interleave(_frontend(x_ref, ng_ref, win_ref, cw_ref, cb_ref, bd_ref.at[0], bd_ref.at[1],
                          bd_ref.at[2], wg_ref, bg_ref, xpad_ref, dst, ns=ns, seg=seg))
    for j in range(CONV_W - 1):
        conv_ref[j] = xpad_ref[:, pad + seg - (CONV_W - 1) + j, :]


HEAD_GROUP = 2


def _sgu_recurrence_kernel(*refs, tm, seg):
    n_hand = len(_HANDOFF)
    sgu_refs, rest = refs[:8], refs[8:]
    xres_ref = rest[0]
    src = dict(zip(_HANDOFF_NAMES, rest[1:1 + n_hand]))
    (c0_ref, n0_ref, m0_ref, hng_ref, skip_ref, wout_ref, fg_ref,
     y_ref, ys_ref, c_ref, n_ref, m_ref, acc_ref) = rest[1 + n_hand:]
    i = pl.program_id(0)
    head0 = (i % (HEADS // HEAD_GROUP)) * HEAD_GROUP

    @pl.when(i == 0)
    def _():
        acc_ref[...] = jnp.zeros(acc_ref.shape, F32)

    _interleave(
        _backend(xres_ref, src, c0_ref, n0_ref.at[0], m0_ref, hng_ref, skip_ref, wout_ref,
                 fg_ref, ys_ref, c_ref, n_ref.at[0], m_ref, ns=1, seg=seg,
                 heads=HEAD_GROUP, head0=head0, acc_ref=acc_ref, memory_read_first=True),
        _sgu_stages(*sgu_refs, y_ref, (), tm=tm))


def _state_specs(ns, index):
    dh = HEAD_DIM
    return [pl.BlockSpec((ns, HEADS, dh, dh), lambda *i: (index(*i), 0, 0, 0)),
            pl.BlockSpec((ns, HEADS, dh), lambda *i: (index(*i), 0, 0)),
            pl.BlockSpec((ns, 1, LANES), lambda *i: (index(*i), 0, 0))]


def _state_shapes(n_streams):
    dh = HEAD_DIM
    return [jax.ShapeDtypeStruct((n_streams, HEADS, dh, dh), F32),
            jax.ShapeDtypeStruct((n_streams, HEADS, dh), F32),
            jax.ShapeDtypeStruct((n_streams, 1, LANES), F32)]


def _front_weight_args(w):
    return [w['norm_g'], w['w_in'], w['conv_w'], w['conv_b'], w['bd'], w['wg'], w['bg']]


def _front_weight_specs(w):
    return [_const_spec(a.shape) for a in _front_weight_args(w)]


def _mlstm_prompt(x, w, *, n_streams, seg):
    e = D_INNER
    n_tiles = x.shape[0] // (n_streams * seg)
    total = n_streams * n_tiles

    def front_tile(t):
        return jnp.minimum(t, total - 1)

    def back_tile(t):
        return jnp.maximum(t - 1, 0)

    cur_spec = pl.BlockSpec((seg, D_MODEL), lambda t: (front_tile(t), 0))
    prev_spec = pl.BlockSpec((seg, D_MODEL), lambda t: (back_tile(t), 0))
    conv_spec = pl.BlockSpec((1, CONV_W - 1, e), lambda t: (front_tile(t) // n_tiles, 0, 0))
    back_specs = [_const_spec((1, e)), _const_spec((1, e)), _const_spec(w['w_out'].shape)]
    fw = _front_weight_specs(w)
    in_specs = [cur_spec, prev_spec, fw[0], _const_spec((1, D_MODEL))] + fw[1:] + back_specs
    fa = _front_weight_args(w)
    args = [x, x, fa[0], w['final_g']] + fa[1:] + [w['hn_g'], w['skip'], w['w_out']]
    handoff = [pltpu.VMEM((seg, width), dtype)
               for _ in range(2) for _, width, dtype in _HANDOFF]
    return pl.pallas_call(
        functools.partial(_mlstm_pipelined_kernel, seg=seg, n_tiles=n_tiles),
        grid=(total + 1,),
        in_specs=in_specs,
        out_specs=([prev_spec] + _state_specs(1, lambda t: back_tile(t) // n_tiles)
                   + [conv_spec]),
        out_shape=([jax.ShapeDtypeStruct(x.shape, F32)] + _state_shapes(n_streams)
                   + [jax.ShapeDtypeStruct((n_streams, CONV_W - 1, e), F32)]),
        scratch_shapes=[pltpu.VMEM((1, SUBLANES + seg, e), F32)] + handoff,
        compiler_params=pltpu.CompilerParams(
            dimension_semantics=("arbitrary",),
            vmem_limit_bytes=VMEM_LIMIT_BYTES),
        name="mlstm_prompt",
    )(*args)


def _mlstm_sample_front(x, w, conv0, *, seg, ns_front):
    e = D_INNER
    n_streams = conv0.shape[1]
    rows_f = ns_front * seg
    hand_shapes = [jax.ShapeDtypeStruct((x.shape[0], width), dtype)
                   for _, width, dtype in _HANDOFF]
    conv_spec = pl.BlockSpec((CONV_W - 1, ns_front, e), lambda i: (0, i, 0))
    res = pl.pallas_call(
        functools.partial(_mlstm_front_kernel, ns=ns_front, seg=seg),
        grid=(n_streams // ns_front,),
        in_specs=([pl.BlockSpec((rows_f, D_MODEL), lambda i: (i, 0))]
                  + _front_weight_specs(w) + [conv_spec]),
        out_specs=([pl.BlockSpec((rows_f, width), lambda i: (i, 0)) for _, width, _ in _HANDOFF]
                   + [conv_spec]),
        out_shape=hand_shapes + [jax.ShapeDtypeStruct((CONV_W - 1, n_streams, e), F32)],
        scratch_shapes=[pltpu.VMEM((ns_front, SUBLANES + seg, e), F32)],
        compiler_params=pltpu.CompilerParams(
            dimension_semantics=("arbitrary",), vmem_limit_bytes=VMEM_LIMIT_BYTES),
        name="mlstm_sample_front",
    )(x, *_front_weight_args(w), conv0)
    return res[:-1], res[-1]


def _sgu_prompt_with_sample_recurrence(xp, sgu_args, xs, hand, state, w, *, seg):
    e, dh = D_INNER, HEAD_DIM
    c0, n0, m0 = state
    n_streams = c0.shape[0]
    groups = HEADS // HEAD_GROUP
    tm = MXU_DIM
    steps = xp.shape[0] // tm
    assert steps == n_streams * groups, (steps, n_streams, groups)
    gw = HEAD_GROUP * dh
    norm_g, w_in, ln_g, ln_b, w_s, b_s, w_out = sgu_args

    def stream(i):
        return i // groups

    def group(i):
        return i % groups

    sgu_specs = [pl.BlockSpec((tm, D_MODEL), lambda i: (i, 0)),
                 _const_spec((1, D_MODEL)), _const_spec(w_in.shape),
                 _const_spec((1, e)), _const_spec((1, e)),
                 _const_spec((SGU_GROUPS, SGU_CHUNK, SGU_CHUNK)),
                 _const_spec((SGU_CHUNK, SGU_GROUPS)), _const_spec(w_out.shape)]
    hand_specs = [pl.BlockSpec((seg, gw), lambda i: (stream(i), group(i))) if width == e
                  else pl.BlockSpec((seg, width), lambda i: (stream(i), 0))
                  for _, width, _ in _HANDOFF]
    c_spec = pl.BlockSpec((1, HEAD_GROUP, dh, dh), lambda i: (stream(i), group(i), 0, 0))
    n_spec = pl.BlockSpec((1, 1, HEAD_GROUP, dh), lambda i: (stream(i), group(i), 0, 0))
    m_spec = pl.BlockSpec((1, 1, LANES), lambda i: (stream(i), 0, 0))
    ys_spec = pl.BlockSpec((seg, D_MODEL), lambda i: (stream(i), 0))
    head_vec_spec = pl.BlockSpec((1, gw), lambda i: (0, group(i)))
    wout_spec = pl.BlockSpec((D_MODEL // MXU_DIM, gw, MXU_DIM), lambda i: (0, group(i), 0))
    n0g = n0.reshape(n_streams, groups, HEAD_GROUP, dh)
    xp1, ys, c_out, n_out, m_out = pl.pallas_call(
        functools.partial(_sgu_recurrence_kernel, tm=tm, seg=seg),
        grid=(steps,),
        in_specs=(sgu_specs + [ys_spec] + hand_specs + [c_spec, n_spec, m_spec]
                  + [head_vec_spec, head_vec_spec, wout_spec, _const_spec((1, D_MODEL))]),
        out_specs=[pl.BlockSpec((tm, D_MODEL), lambda i: (i, 0)), ys_spec,
                   c_spec, n_spec, m_spec],
        out_shape=[jax.ShapeDtypeStruct(xp.shape, F32), jax.ShapeDtypeStruct(xs.shape, F32),
                   jax.ShapeDtypeStruct(c0.shape, F32), jax.ShapeDtypeStruct(n0g.shape, F32),
                   jax.ShapeDtypeStruct(m0.shape, F32)],
        scratch_shapes=[pltpu.VMEM((seg, D_MODEL), F32)],
        compiler_params=pltpu.CompilerParams(
            dimension_semantics=("arbitrary",), vmem_limit_bytes=VMEM_LIMIT_BYTES),
        name="sgu_prompt_sample_recurrence",
    )(xp, norm_g, w_in, ln_g, ln_b, w_s, b_s, w_out, xs, *hand, c0, n0g, m0,
      w['hn_g'], w['skip'], w['w_out'], w['final_g'])
    return xp1, ys, c_out, n_out.reshape(n0.shape), m_out


def _qkv_weights_kernel(rows_ref, wg_ref, bd_ref, fold_ref):
    nt = rows_ref.shape[1]
    r_blk = lax.broadcasted_iota(jnp.int32, (MXU_DIM, MXU_DIM), 0) // QKV_BLOCK
    c_idx = lax.broadcasted_iota(jnp.int32, (MXU_DIM, MXU_DIM), 1)
    same_block = r_blk == c_idx // QKV_BLOCK
    for j in range(nt):
        r0, r1 = j * MXU_DIM, (j + 1) * MXU_DIM
        tiles = []
        for a in range(3):
            rows = rows_ref[a, j]
            tiled = jnp.zeros((MXU_DIM, MXU_DIM), F32)
            for o in range(QKV_BLOCK):
                tiled = jnp.where(c_idx % QKV_BLOCK == o, rows[:, o:o + 1], tiled)
            tile = jnp.where(same_block, tiled, 0.0).astype(BF16)
            bd_ref[a, j] = tile
            tiles.append(tile)
        fold_ref[0, r0:r1, :] = (_dot(tiles[0], wg_ref[0, r0:r1, :])
                                 + _dot(tiles[1], wg_ref[1, r0:r1, :])).astype(BF16)
        fold_ref[1, r0:r1, :] = _dot(tiles[2], wg_ref[2, r0:r1, :]).astype(BF16)


def _qkv_weights(wq, wk, wv, wg):
    nt = D_INNER // MXU_DIM
    rows = jnp.stack([wq, wk, wv]).reshape(3, nt, MXU_DIM, QKV_BLOCK)
    return pl.pallas_call(
        _qkv_weights_kernel,
        out_shape=[jax.ShapeDtypeStruct((3, nt, MXU_DIM, MXU_DIM), BF16),
                   jax.ShapeDtypeStruct((2, D_INNER, wg.shape[-1]), BF16)],
        name="qkv_weights",
    )(rows, wg)


def _head_lanes(w):
    lead = w.shape[:-1]
    z = jnp.zeros(lead + (LANES - HEADS,), w.dtype)
    return jnp.concatenate([w[..., :HEADS], z, w[..., HEADS:], z], axis=-1)


def kernel(x_prompt, x_sample, state_mlstm_C, state_mlstm_n, state_mlstm_m, state_mlstm_conv, norm_g, final_norm_g, a_w_in, a_ln_g, a_ln_b, a_w_s, a_b_s, a_w_out, b_w_in, b_conv_w, b_conv_b, b_wq, b_wk, b_wv, b_w_gates, b_b_gates, b_hnorm_g, b_skip, b_w_out):
    bsz, seq, d = x_prompt.shape
    dec_b, dec_t, _ = x_sample.shape
    e = D_INNER
    assert norm_g.shape[0] == 2 and a_w_in.shape[0] == 1 and b_w_in.shape[0] == 1
    assert d == D_MODEL and seq % MXU_DIM == 0 and SGU_CHUNK % dec_t == 0

    xp = x_prompt.reshape(bsz * seq, d)
    xs = x_sample.reshape(dec_b * dec_t, d)

    a_in = _col_slabs(a_w_in[0])
    a_out = _col_slabs(a_w_out[0])
    ng0 = norm_g[0].reshape(1, d)
    lng = a_ln_g[0].reshape(1, e)
    lnb = a_ln_b[0].reshape(1, e)
    per_tile = SGU_CHUNK // dec_t
    w_head = a_w_s[0][:, :dec_t, :dec_t]
    ws_s = jnp.einsum('ab,gts->gatbs', np.eye(per_tile, dtype=np.float32), w_head)
    ws_s = ws_s.reshape(SGU_GROUPS, SGU_CHUNK, SGU_CHUNK)
    bs_s = jnp.tile(a_b_s[0][:, :dec_t], (1, per_tile)).T
    xs1, v_s = _sgu_layer(xs, ng0, a_in, lng, lnb, ws_s, bs_s, a_out,
                          tm=2 * SGU_CHUNK, want_v=True)

    wg = _head_lanes(b_w_gates[0]).astype(BF16).reshape(3, e, 2 * LANES)
    bd, wg_folded = _qkv_weights(b_wq[0], b_wk[0], b_wv[0], wg)
    w = dict(norm_g=norm_g[1].reshape(1, d), final_g=final_norm_g.reshape(1, d),
             w_in=_col_slabs(b_w_in[0]), conv_w=b_conv_w[0], conv_b=b_conv_b[0].reshape(1, e),
             bd=bd, wg=wg_folded,
             bg=_head_lanes(b_b_gates[0]).reshape(1, 2 * LANES),
             hn_g=b_hnorm_g[0].reshape(1, e), skip=b_skip[0].reshape(1, e),
             w_out=_col_slabs(b_w_out[0]))
    hand, conv_s = _mlstm_sample_front(xs1, w, state_mlstm_conv[0].transpose(1, 0, 2),
                                       seg=dec_t, ns_front=dec_b // 2)
    conv_s = conv_s.transpose(1, 0, 2)
    m0 = jnp.pad(state_mlstm_m[0], ((0, 0), (0, LANES - HEADS))).reshape(dec_b, 1, LANES)
    xp1, ys, c_s, n_s, m_s = _sgu_prompt_with_sample_recurrence(
        xp, (ng0, a_in, lng, lnb, a_w_s[0], a_b_s[0].T, a_out), xs1, hand,
        (state_mlstm_C[0], state_mlstm_n[0], m0), w, seg=dec_t)
    yp, c_p, n_p, m_p, conv_p = _mlstm_prompt(xp1, w, n_streams=bsz, seg=MXU_DIM)

    return (yp.reshape(bsz, seq, d),
            ys.reshape(dec_b, dec_t, d),
            v_s.reshape(1, dec_b, dec_t, e),
            c_p[None], n_p[None], m_p[:, 0, :HEADS][None], conv_p[None],
            c_s[None], n_s[None], m_s[:, 0, :HEADS][None], conv_s[None])
```

```python
import functools
import math

import numpy as np
import jax
import jax.numpy as jnp
from jax import lax
from jax.experimental import pallas as pl
from jax.experimental.pallas import tpu as pltpu

D_MODEL = 1024
D_INNER = 2048
SGU_CHUNK = 128
SGU_GROUPS = 8
SGU_GROUP_DIM = D_INNER // SGU_GROUPS
HEADS = 4
HEAD_DIM = D_INNER // HEADS
QKV_BLOCK = 4
CONV_W = 4
RMS_EPS = 1e-6
LN_EPS = 1e-5

LANES = 128
SUBLANES = 8
MXU_DIM = 256
VMEM_LIMIT_BYTES = 60 * 1024 * 1024
SLAB_BLOCK_BYTES = 6 * 1024 * 1024

BF16 = jnp.bfloat16
F32 = jnp.float32
LOG2E = math.log2(math.e)

assert SGU_GROUP_DIM == MXU_DIM


def _dot(a, b):
    return jnp.dot(a, b, preferred_element_type=F32)


def _rmsnorm(x, g):
    return x * lax.rsqrt(jnp.mean(x * x, axis=-1, keepdims=True) + RMS_EPS) * g


def _log_sigmoid(x):
    return jnp.minimum(x, 0.0) - jnp.log1p(jnp.exp(-jnp.abs(x)))


def _prefix_rows(x, op, fill):
    n = x.shape[0]
    row = lax.broadcasted_iota(jnp.int32, x.shape, 0)
    shift = 1
    while shift < n:
        if shift % SUBLANES == 0:
            shifted = jnp.concatenate(
                [jnp.full((shift, x.shape[1]), fill, x.dtype), x[:n - shift]], axis=0)
        else:
            shifted = jnp.where(row >= shift, pltpu.roll(x, shift, 0), fill)
        x = op(x, shifted)
        shift *= 2
    return x


def _const_spec(shape):
    zeros = (0,) * len(shape)
    return pl.BlockSpec(shape, lambda *_: zeros, pipeline_mode=pl.Buffered(1))


def _slab_kernel(w_ref, o_ref):
    for j in range(o_ref.shape[0]):
        o_ref[j] = w_ref[:, j * MXU_DIM:(j + 1) * MXU_DIM].astype(BF16)


def _col_slabs(w):
    k, n = w.shape
    rows = MXU_DIM
    while 2 * rows <= k and 2 * rows * n * 4 <= SLAB_BLOCK_BYTES:
        rows *= 2
    return pl.pallas_call(
        _slab_kernel,
        grid=(k // rows,),
        in_specs=[pl.BlockSpec((rows, n), lambda i: (i, 0))],
        out_specs=pl.BlockSpec((n // MXU_DIM, rows, MXU_DIM), lambda i: (0, i, 0)),
        out_shape=jax.ShapeDtypeStruct((n // MXU_DIM, k, MXU_DIM), BF16),
        compiler_params=pltpu.CompilerParams(dimension_semantics=("arbitrary",)),
        name="weight_slabs",
    )(w)


def _sgu_kernel(x_ref, ng_ref, win_ref, lng_ref, lnb_ref, ws_ref, bs_ref, wout_ref,
                y_ref, *v_refs, tm):
    _interleave(_sgu_stages(x_ref, ng_ref, win_ref, lng_ref, lnb_ref, ws_ref, bs_ref, wout_ref,
                            y_ref, v_refs, tm=tm))


def _sgu_stages(x_ref, ng_ref, win_ref, lng_ref, lnb_ref, ws_ref, bs_ref, wout_ref,
                y_ref, v_refs, *, tm):
    ng = SGU_GROUPS
    e = D_INNER
    x = x_ref[...]
    xn = _rmsnorm(x, ng_ref[...]).astype(BF16)

    v_g, uz_g = [], []
    row_sum = jnp.zeros((tm, 1), F32)
    for g in range(ng):
        v = jax.nn.gelu(_dot(xn, win_ref[ng + g]))
        row_sum = row_sum + jnp.sum(v, axis=1, keepdims=True)
        v_g.append(v)
        if g % 4 == 3:
            yield
    mu = row_sum * (1.0 / e)
    sq_sum = jnp.zeros((tm, 1), F32)
    for g in range(ng):
        u = jax.nn.gelu(_dot(xn, win_ref[g]))
        z = _dot(xn, win_ref[2 * ng + g])
        uz_g.append(u * jax.nn.silu(z))
        v_g[g] = v_g[g] - mu
        sq_sum = sq_sum + jnp.sum(v_g[g] * v_g[g], axis=1, keepdims=True)
        if g % 4 == 3:
            yield
    rstd = lax.rsqrt(sq_sum * (1.0 / e) + LN_EPS)

    ti = lax.broadcasted_iota(jnp.int32, (SGU_CHUNK, SGU_CHUNK), 0)
    si = lax.broadcasted_iota(jnp.int32, (SGU_CHUNK, SGU_CHUNK), 1)
    causal = si <= ti
    s_g = []
    for g in range(ng):
        lo, hi = g * SGU_GROUP_DIM, (g + 1) * SGU_GROUP_DIM
        vn = v_g[g] * rstd * lng_ref[:, lo:hi] + lnb_ref[:, lo:hi]
        if v_refs:
            v_refs[0][:, lo:hi] = vn
        vb = vn.astype(BF16)
        wm = jnp.where(causal, ws_ref[g], 0.0).astype(BF16)
        bias = bs_ref[:, g:g + 1]
        s_g.append(jnp.concatenate(
            [_dot(wm, vb[c * SGU_CHUNK:(c + 1) * SGU_CHUNK]) + bias
             for c in range(tm // SGU_CHUNK)], axis=0))
    yield
    n_out = D_MODEL // MXU_DIM
    acc = [None] * n_out
    for g in range(ng):
        lo, hi = g * SGU_GROUP_DIM, (g + 1) * SGU_GROUP_DIM
        out = (uz_g[g] * s_g[g]).astype(BF16)
        for n in range(n_out):
            part = _dot(out, wout_ref[n, lo:hi, :])
            acc[n] = part if acc[n] is None else acc[n] + part
        if g == ng // 2 - 1:
            yield
    y_ref[...] = x + jnp.concatenate(acc, axis=1)


def _sgu_layer(x, norm_g, w_in, ln_g, ln_b, w_s, b_s, w_out, *, tm, want_v):
    rows = x.shape[0]
    e = D_INNER
    row_spec = pl.BlockSpec((tm, D_MODEL), lambda i: (i, 0))
    out_shape = [jax.ShapeDtypeStruct((rows, D_MODEL), F32)]
    out_specs = [row_spec]
    if want_v:
        out_shape.append(jax.ShapeDtypeStruct((rows, e), F32))
        out_specs.append(pl.BlockSpec((tm, e), lambda i: (i, 0)))
    res = pl.pallas_call(
        functools.partial(_sgu_kernel, tm=tm),
        grid=(rows // tm,),
        in_specs=[row_spec,
                  _const_spec((1, D_MODEL)),
                  _const_spec(w_in.shape),
                  _const_spec((1, e)),
                  _const_spec((1, e)),
                  _const_spec((SGU_GROUPS, SGU_CHUNK, SGU_CHUNK)),
                  _const_spec((SGU_CHUNK, SGU_GROUPS)),
                  _const_spec(w_out.shape)],
        out_specs=out_specs,
        out_shape=out_shape,
        compiler_params=pltpu.CompilerParams(
            dimension_semantics=("arbitrary",), vmem_limit_bytes=VMEM_LIMIT_BYTES),
        name="sgu_layer_v" if want_v else "sgu_layer",
    )(x, norm_g, w_in, ln_g, ln_b, w_s, b_s, w_out)
    return res


def _cols_to_rows(a, seg):
    if seg % LANES == 0:
        return a.T[0:SUBLANES, :]
    ti = lax.broadcasted_iota(jnp.int32, (seg, seg), 0)
    si = lax.broadcasted_iota(jnp.int32, (seg, seg), 1)
    return jnp.concatenate(
        [jnp.sum(jnp.where(ti == si, a[:, h:h + 1], 0.0), axis=0, keepdims=True)
         for h in range(HEADS)], axis=0)


_HANDOFF = (('xc', D_INNER, F32), ('sz', D_INNER, F32), ('qb', D_INNER, BF16),
            ('ks', D_INNER, F32), ('vb', D_INNER, BF16), ('gates', 2 * LANES, F32))
_HANDOFF_NAMES = tuple(name for name, _, _ in _HANDOFF)


def _interleave(*gens):
    live = list(gens)
    while live:
        for gen in list(live):
            try:
                next(gen)
            except StopIteration:
                live.remove(gen)


def _frontend(x_ref, ng_ref, win_ref, cw_ref, cb_ref, bdq_ref, bdk_ref, bdv_ref, wg_ref,
              bg_ref, xpad_ref, dst, *, ns, seg):
    e, dh = D_INNER, HEAD_DIM
    pad = SUBLANES
    rows = ns * seg
    tiles_per_head = dh // MXU_DIM
    xn = _rmsnorm(x_ref[...], ng_ref[...]).astype(BF16)

    def project(h):
        t0 = h * tiles_per_head
        xm_ = jnp.concatenate(
            [_dot(xn, win_ref[t0 + j]) for j in range(tiles_per_head)], axis=1)
        z_ = jnp.concatenate(
            [_dot(xn, win_ref[e // MXU_DIM + t0 + j]) for j in range(tiles_per_head)], axis=1)
        return xm_, z_

    projected = project(0)
    gates = bg_ref[...]
    yield
    for h in range(HEADS):
        c0, c1 = h * dh, (h + 1) * dh
        t0 = h * tiles_per_head
        xm, z = projected
        if h + 1 < HEADS:
            projected = project(h + 1)
        xpad_ref[:, pad:pad + seg, c0:c1] = xm.reshape(ns, seg, dh)
        pre = cb_ref[:, c0:c1] + cw_ref[CONV_W - 1:CONV_W, c0:c1] * xm
        for d in range(1, CONV_W):
            shifted = xpad_ref[:, pad - d:pad - d + seg, c0:c1].reshape(rows, dh)
            pre = pre + cw_ref[CONV_W - 1 - d:CONV_W - d, c0:c1] * shifted
        xc = jax.nn.silu(pre)
        xcb = xc.astype(BF16)
        xmb = xm.astype(BF16)

        def blockdiag(a, w_ref):
            return jnp.concatenate(
                [_dot(a[:, j * MXU_DIM:(j + 1) * MXU_DIM], w_ref[t0 + j])
                 for j in range(tiles_per_head)], axis=1)

        q = blockdiag(xcb, bdq_ref)
        k = blockdiag(xcb, bdk_ref)
        v = blockdiag(xmb, bdv_ref)
        qb, vb = q.astype(BF16), v.astype(BF16)
        gates = gates + _dot(xcb, wg_ref[0, c0:c1, :]) + _dot(xmb, wg_ref[1, c0:c1, :])
        dst['xc'][:, c0:c1] = xc
        dst['sz'][:, c0:c1] = jax.nn.silu(z)
        dst['qb'][:, c0:c1] = qb
        dst['ks'][:, c0:c1] = k * (dh ** -0.5)
        dst['vb'][:, c0:c1] = vb
        if h == HEADS - 1:
            dst['gates'][...] = gates
        yield


def _backend(xres_ref, src, cin_ref, nin_ref, min_ref, hng_ref, skip_ref, wout_ref, fg_ref,
             y_ref, c_ref, n_ref, m_ref, *, ns, seg, memory_read_first=False):
    dh = HEAD_DIM
    n_out = D_MODEL // MXU_DIM
    ti = lax.broadcasted_iota(jnp.int32, (seg, seg), 0)
    si = lax.broadcasted_iota(jnp.int32, (seg, seg), 1)
    causal = si <= ti
    pairs = [(s, h) for s in range(ns) for h in range(HEADS)]

    def block(name, s, h):
        return src[name][s * seg:(s + 1) * seg, h * dh:(h + 1) * dh]

    qc, cprev = {}, {}
    if memory_read_first:
        qc = {(s, h): _dot(block('qb', s, h), cin_ref[s, h].astype(BF16)) for s, h in pairs}
    gt = []
    for s in range(ns):
        g = src['gates'][s * seg:(s + 1) * seg, :]
        ig = g[:, 0:LANES]
        lf = _log_sigmoid(g[:, LANES:2 * LANES])
        b = _prefix_rows(lf, jnp.add, 0.0)
        a = ig - b
        cm = _prefix_rows(a, jnp.maximum, -jnp.inf)
        m_prev = jnp.concatenate(
            [min_ref[s], jnp.zeros((1, LANES - HEADS), F32)], axis=1)
        neg_mt_plus_b = -jnp.maximum(m_prev, cm)
        b_last = b[seg - 1:seg, :]
        m_new = b_last + jnp.maximum(m_prev, cm[seg - 1:seg, :])
        gt.append(dict(
            w_inter=jnp.exp(m_prev + neg_mt_plus_b),
            inv_floor=jnp.exp(neg_mt_plus_b - b),
            w=jnp.exp(b_last + a - m_new),
            decay=jnp.exp(b_last + m_prev - m_new),
            a_rows=_cols_to_rows(a, seg) * LOG2E,
            u=neg_mt_plus_b * LOG2E,
            n_prev=nin_ref[s]))
        m_ref[s] = m_new[:, 0:HEADS]
    yield

    sc = {}
    for s, h in pairs:
        w_intra = jnp.where(
            causal, jnp.exp2(gt[s]['u'][:, h:h + 1] + gt[s]['a_rows'][h:h + 1, :]), 0.0)
        sc[s, h] = lax.dot_general(block('qb', s, h), block('ks', s, h).astype(BF16),
                                   (((1,), (1,)), ((), ())),
                                   preferred_element_type=F32) * w_intra
        if not memory_read_first:
            cprev[s, h] = cin_ref[s, h]
            qc[s, h] = _dot(block('qb', s, h), cprev[s, h].astype(BF16))
    yield

    num = {(s, h): _dot(sc[s, h].astype(BF16), block('vb', s, h))
           + gt[s]['w_inter'][:, h:h + 1] * qc[s, h] for s, h in pairs}
    yield

    out, n_new = {}, {}
    for s, h in pairs:
        c0, c1 = h * dh, (h + 1) * dh
        n_prev = gt[s]['n_prev'][h:h + 1, :]
        qn = jnp.sum(block('qb', s, h).astype(F32) * n_prev, axis=1, keepdims=True)
        den = jnp.sum(sc[s, h], axis=1, keepdims=True) + gt[s]['w_inter'][:, h:h + 1] * qn
        r = 1.0 / jnp.maximum(jnp.abs(den), gt[s]['inv_floor'][:, h:h + 1])
        cen = num[s, h] - jnp.mean(num[s, h], axis=1, keepdims=True)
        var = jnp.mean(cen * cen, axis=1, keepdims=True)
        hn = cen * (r * lax.rsqrt(r * r * var + LN_EPS)) * hng_ref[:, c0:c1]
        out[s, h] = ((hn + skip_ref[:, c0:c1] * block('xc', s, h))
                     * block('sz', s, h)).astype(BF16)

        decay = gt[s]['decay'][:, h:h + 1]
        kw = block('ks', s, h) * gt[s]['w'][:, h:h + 1]
        upd = lax.dot_general(kw.astype(BF16), block('vb', s, h), (((0,), (0,)), ((), ())),
                              preferred_element_type=F32)
        c_old = cin_ref[s, h] if memory_read_first else cprev[s, h]
        c_ref[s, h] = decay * c_old + upd
        n_new[s, h] = decay * n_prev + jnp.sum(kw, axis=0, keepdims=True)
    for s in range(ns):
        n_ref[s] = jnp.concatenate([n_new[s, h] for h in range(HEADS)], axis=0)
    yield

    y_rows = []
    for s in range(ns):
        acc = [None] * n_out
        for h in range(HEADS):
            for n in range(n_out):
                part = _dot(out[s, h], wout_ref[n, h * dh:(h + 1) * dh, :])
                acc[n] = part if acc[n] is None else acc[n] + part
        y_rows.append(jnp.concatenate(acc, axis=1))
    proj = y_rows[0] if ns == 1 else jnp.concatenate(y_rows, axis=0)
    y_ref[...] = _rmsnorm(xres_ref[...] + proj, fg_ref[...])


def _mlstm_pipelined_kernel(x_ref, xprev_ref, ng_ref, fg_ref, win_ref, cw_ref, cb_ref,
                            bd_ref, wg_ref, bg_ref, hng_ref, skip_ref,
                            wout_ref, y_ref, c_ref, n_ref, m_ref, conv_ref, xpad_ref,
                            *handoff, seg, n_tiles, total):
    t = pl.program_id(0)
    e = D_INNER
    pad = SUBLANES
    n_hand = len(_HANDOFF)
    sets = (dict(zip(_HANDOFF_NAMES, handoff[:n_hand])),
            dict(zip(_HANDOFF_NAMES, handoff[n_hand:])))

    @pl.when(t % n_tiles == 0)
    def _():
        xpad_ref[:, 0:pad, :] = jnp.zeros((1, pad, e), F32)

    @pl.when((t - 1) % n_tiles == 0)
    def _():
        c_ref[...] = jnp.zeros(c_ref.shape, F32)
        n_ref[...] = jnp.zeros(n_ref.shape, F32)
        m_ref[...] = jnp.zeros(m_ref.shape, F32)

    def step(dst, src):
        gens = []
        if src is not None:
            gens.append(_backend(xprev_ref, src, c_ref, n_ref, m_ref, hng_ref, skip_ref,
                                 wout_ref, fg_ref, y_ref, c_ref, n_ref, m_ref, ns=1, seg=seg))
        if dst is not None:
            gens.append(_frontend(x_ref, ng_ref, win_ref, cw_ref, cb_ref, bd_ref.at[0],
                                  bd_ref.at[1], bd_ref.at[2], wg_ref, bg_ref, xpad_ref, dst,
                                  ns=1, seg=seg))
        _interleave(*gens)
        if dst is not None:
            conv_ref[...] = xpad_ref[:, pad + seg - (CONV_W - 1):pad + seg, :]
            xpad_ref[:, 0:pad, :] = xpad_ref[:, seg:seg + pad, :]

    @pl.when(t == 0)
    def _():
        step(sets[0], None)

    @pl.when(t == total)
    def _():
        step(None, sets[(total - 1) % 2])

    inner = (t > 0) & (t < total)

    @pl.when(inner & (t % 2 == 0))
    def _():
        step(sets[0], sets[1])

    @pl.when(inner & (t % 2 == 1))
    def _():
        step(sets[1], sets[0])


def _mlstm_front_kernel(x_ref, ng_ref, win_ref, cw_ref, cb_ref, bd_ref, wg_ref, bg_ref,
                        conv0_ref, *rest, ns, seg):
    n_hand = len(_HANDOFF)
    dst = dict(zip(_HANDOFF_NAMES, rest[:n_hand]))
    conv_ref, xpad_ref = rest[n_hand:]
    pad = SUBLANES
    for j in range(CONV_W - 1):
        xpad_ref[:, pad - (CONV_W - 1) + j, :] = conv0_ref[j]
    _interleave(_frontend(x_ref, ng_ref, win_ref, cw_ref, cb_ref, bd_ref.at[0], bd_ref.at[1],
                          bd_ref.at[2], wg_ref, bg_ref, xpad_ref, dst, ns=ns, seg=seg))
    for j in range(CONV_W - 1):
        conv_ref[j] = xpad_ref[:, pad + seg - (CONV_W - 1) + j, :]


def _sgu_recurrence_kernel(*refs, tm, seg):
    n_hand = len(_HANDOFF)
    sgu_refs, rest = refs[:8], refs[8:]
    xres_ref = rest[0]
    src = dict(zip(_HANDOFF_NAMES, rest[1:1 + n_hand]))
    (c0_ref, n0_ref, m0_ref, hng_ref, skip_ref, wout_ref, fg_ref,
     y_ref, ys_ref, c_ref, n_ref, m_ref) = rest[1 + n_hand:]
    _interleave(
        _backend(xres_ref, src, c0_ref, n0_ref, m0_ref, hng_ref, skip_ref, wout_ref,
                 fg_ref, ys_ref, c_ref, n_ref, m_ref, ns=1, seg=seg, memory_read_first=True),
        _sgu_stages(*sgu_refs, y_ref, (), tm=tm))


def _state_specs(ns, index):
    dh = HEAD_DIM
    return [pl.BlockSpec((ns, HEADS, dh, dh), lambda *i: (index(*i), 0, 0, 0)),
            pl.BlockSpec((ns, HEADS, dh), lambda *i: (index(*i), 0, 0)),
            pl.BlockSpec((ns, 1, HEADS), lambda *i: (index(*i), 0, 0))]


def _state_shapes(n_streams):
    dh = HEAD_DIM
    return [jax.ShapeDtypeStruct((n_streams, HEADS, dh, dh), F32),
            jax.ShapeDtypeStruct((n_streams, HEADS, dh), F32),
            jax.ShapeDtypeStruct((n_streams, 1, HEADS), F32)]


def _front_weight_args(w):
    return [w['norm_g'], w['w_in'], w['conv_w'], w['conv_b'], w['bd'], w['wg'], w['bg']]


def _front_weight_specs(w):
    return [_const_spec(a.shape) for a in _front_weight_args(w)]


def _mlstm_prompt(x, w, *, n_streams, seg):
    e = D_INNER
    n_tiles = x.shape[0] // (n_streams * seg)
    total = n_streams * n_tiles

    def front_tile(t):
        return jnp.minimum(t, total - 1)

    def back_tile(t):
        return jnp.maximum(t - 1, 0)

    cur_spec = pl.BlockSpec((seg, D_MODEL), lambda t: (front_tile(t), 0))
    prev_spec = pl.BlockSpec((seg, D_MODEL), lambda t: (back_tile(t), 0))
    conv_spec = pl.BlockSpec((1, CONV_W - 1, e), lambda t: (front_tile(t) // n_tiles, 0, 0))
    back_specs = [_const_spec((1, e)), _const_spec((1, e)), _const_spec(w['w_out'].shape)]
    fw = _front_weight_specs(w)
    in_specs = [cur_spec, prev_spec, fw[0], _const_spec((1, D_MODEL))] + fw[1:] + back_specs
    fa = _front_weight_args(w)
    args = [x, x, fa[0], w['final_g']] + fa[1:] + [w['hn_g'], w['skip'], w['w_out']]
    handoff = [pltpu.VMEM((seg, width), dtype)
               for _ in range(2) for _, width, dtype in _HANDOFF]
    return pl.pallas_call(
        functools.partial(_mlstm_pipelined_kernel, seg=seg, n_tiles=n_tiles, total=total),
        grid=(total + 1,),
        in_specs=in_specs,
        out_specs=([prev_spec] + _state_specs(1, lambda t: back_tile(t) // n_tiles)
                   + [conv_spec]),
        out_shape=([jax.ShapeDtypeStruct(x.shape, F32)] + _state_shapes(n_streams)
                   + [jax.ShapeDtypeStruct((n_streams, CONV_W - 1, e), F32)]),
        scratch_shapes=[pltpu.VMEM((1, SUBLANES + seg, e), F32)] + handoff,
        compiler_params=pltpu.CompilerParams(
            dimension_semantics=("arbitrary",),
            vmem_limit_bytes=VMEM_LIMIT_BYTES),
        name="mlstm_prompt",
    )(*args)


def _mlstm_sample_front(x, w, conv0, *, seg, ns_front):
    e = D_INNER
    n_streams = conv0.shape[1]
    rows_f = ns_front * seg
    hand_shapes = [jax.ShapeDtypeStruct((x.shape[0], width), dtype)
                   for _, width, dtype in _HANDOFF]
    conv_spec = pl.BlockSpec((CONV_W - 1, ns_front, e), lambda i: (0, i, 0))
    res = pl.pallas_call(
        functools.partial(_mlstm_front_kernel, ns=ns_front, seg=seg),
        grid=(n_streams // ns_front,),
        in_specs=([pl.BlockSpec((rows_f, D_MODEL), lambda i: (i, 0))]
                  + _front_weight_specs(w) + [conv_spec]),
        out_specs=([pl.BlockSpec((rows_f, width), lambda i: (i, 0)) for _, width, _ in _HANDOFF]
                   + [conv_spec]),
        out_shape=hand_shapes + [jax.ShapeDtypeStruct((CONV_W - 1, n_streams, e), F32)],
        scratch_shapes=[pltpu.VMEM((ns_front, SUBLANES + seg, e), F32)],
        compiler_params=pltpu.CompilerParams(
            dimension_semantics=("arbitrary",), vmem_limit_bytes=VMEM_LIMIT_BYTES),
        name="mlstm_sample_front",
    )(x, *_front_weight_args(w), conv0)
    return res[:-1], res[-1]


def _sgu_prompt_with_sample_recurrence(xp, sgu_args, xs, hand, state, w, *, seg):
    e = D_INNER
    c0, n0, m0 = state
    n_streams = c0.shape[0]
    assert xp.shape[0] % n_streams == 0 and (xp.shape[0] // n_streams) % SGU_CHUNK == 0
    tm = xp.shape[0] // n_streams
    norm_g, w_in, ln_g, ln_b, w_s, b_s, w_out = sgu_args

    sgu_specs = [pl.BlockSpec((tm, D_MODEL), lambda i: (i, 0)),
                 _const_spec((1, D_MODEL)), _const_spec(w_in.shape),
                 _const_spec((1, e)), _const_spec((1, e)),
                 _const_spec((SGU_GROUPS, SGU_CHUNK, SGU_CHUNK)),
                 _const_spec((SGU_CHUNK, SGU_GROUPS)), _const_spec(w_out.shape)]
    hand_specs = [pl.BlockSpec((seg, width), lambda i: (i, 0)) for _, width, _ in _HANDOFF]
    state_specs = _state_specs(1, lambda i: i)
    ys_spec = pl.BlockSpec((seg, D_MODEL), lambda i: (i, 0))
    xp1, ys, c_out, n_out, m_out = pl.pallas_call(
        functools.partial(_sgu_recurrence_kernel, tm=tm, seg=seg),
        grid=(n_streams,),
        in_specs=(sgu_specs + [ys_spec] + hand_specs + state_specs
                  + [_const_spec((1, e)), _const_spec((1, e)), _const_spec(w['w_out'].shape),
                     _const_spec((1, D_MODEL))]),
        out_specs=[pl.BlockSpec((tm, D_MODEL), lambda i: (i, 0)), ys_spec] + state_specs,
        out_shape=([jax.ShapeDtypeStruct(xp.shape, F32), jax.ShapeDtypeStruct(xs.shape, F32)]
                   + _state_shapes(n_streams)),
        compiler_params=pltpu.CompilerParams(
            dimension_semantics=("arbitrary",), vmem_limit_bytes=VMEM_LIMIT_BYTES),
        name="sgu_prompt_sample_recurrence",
    )(xp, norm_g, w_in, ln_g, ln_b, w_s, b_s, w_out, xs, *hand, c0, n0, m0,
      w['hn_g'], w['skip'], w['w_out'], w['final_g'])
    return xp1, ys, c_out, n_out, m_out


def _qkv_weights_kernel(rows_ref, wg_ref, bd_ref, fold_ref):
    nt = rows_ref.shape[1]
    r_blk = lax.broadcasted_iota(jnp.int32, (MXU_DIM, MXU_DIM), 0) // QKV_BLOCK
    c_idx = lax.broadcasted_iota(jnp.int32, (MXU_DIM, MXU_DIM), 1)
    same_block = r_blk == c_idx // QKV_BLOCK
    for j in range(nt):
        r0, r1 = j * MXU_DIM, (j + 1) * MXU_DIM
        tiles = []
        for a in range(3):
            rows = rows_ref[a, j]
            tiled = jnp.zeros((MXU_DIM, MXU_DIM), F32)
            for o in range(QKV_BLOCK):
                tiled = jnp.where(c_idx % QKV_BLOCK == o, rows[:, o:o + 1], tiled)
            tile = jnp.where(same_block, tiled, 0.0).astype(BF16)
            bd_ref[a, j] = tile
            tiles.append(tile)
        fold_ref[0, r0:r1, :] = (_dot(tiles[0], wg_ref[0, r0:r1, :])
                                 + _dot(tiles[1], wg_ref[1, r0:r1, :])).astype(BF16)
        fold_ref[1, r0:r1, :] = _dot(tiles[2], wg_ref[2, r0:r1, :]).astype(BF16)


def _qkv_weights(wq, wk, wv, wg):
    nt = D_INNER // MXU_DIM
    rows = jnp.stack([wq, wk, wv]).reshape(3, nt, MXU_DIM, QKV_BLOCK)
    return pl.pallas_call(
        _qkv_weights_kernel,
        out_shape=[jax.ShapeDtypeStruct((3, nt, MXU_DIM, MXU_DIM), BF16),
                   jax.ShapeDtypeStruct((2, D_INNER, wg.shape[-1]), BF16)],
        name="qkv_weights",
    )(rows, wg)


def _head_lanes(w):
    lead = w.shape[:-1]
    z = jnp.zeros(lead + (LANES - HEADS,), w.dtype)
    return jnp.concatenate([w[..., :HEADS], z, w[..., HEADS:], z], axis=-1)


def kernel(x_prompt, x_sample, state_mlstm_C, state_mlstm_n, state_mlstm_m, state_mlstm_conv, norm_g, final_norm_g, a_w_in, a_ln_g, a_ln_b, a_w_s, a_b_s, a_w_out, b_w_in, b_conv_w, b_conv_b, b_wq, b_wk, b_wv, b_w_gates, b_b_gates, b_hnorm_g, b_skip, b_w_out):
    bsz, seq, d = x_prompt.shape
    dec_b, dec_t, _ = x_sample.shape
    e = D_INNER
    assert norm_g.shape[0] == 2 and a_w_in.shape[0] == 1 and b_w_in.shape[0] == 1
    assert d == D_MODEL and seq % MXU_DIM == 0 and SGU_CHUNK % dec_t == 0

    xp = x_prompt.reshape(bsz * seq, d)
    xs = x_sample.reshape(dec_b * dec_t, d)

    a_in = _col_slabs(a_w_in[0])
    a_out = _col_slabs(a_w_out[0])
    ng0 = norm_g[0].reshape(1, d)
    lng = a_ln_g[0].reshape(1, e)
    lnb = a_ln_b[0].reshape(1, e)
    per_tile = SGU_CHUNK // dec_t
    w_head = a_w_s[0][:, :dec_t, :dec_t]
    ws_s = jnp.einsum('ab,gts->gatbs', np.eye(per_tile, dtype=np.float32), w_head)
    ws_s = ws_s.reshape(SGU_GROUPS, SGU_CHUNK, SGU_CHUNK)
    bs_s = jnp.tile(a_b_s[0][:, :dec_t], (1, per_tile)).T
    xs1, v_s = _sgu_layer(xs, ng0, a_in, lng, lnb, ws_s, bs_s, a_out,
                          tm=2 * SGU_CHUNK, want_v=True)

    wg = _head_lanes(b_w_gates[0]).astype(BF16).reshape(3, e, 2 * LANES)
    bd, wg_folded = _qkv_weights(b_wq[0], b_wk[0], b_wv[0], wg)
    w = dict(norm_g=norm_g[1].reshape(1, d), final_g=final_norm_g.reshape(1, d),
             w_in=_col_slabs(b_w_in[0]), conv_w=b_conv_w[0], conv_b=b_conv_b[0].reshape(1, e),
             bd=bd, wg=wg_folded,
             bg=_head_lanes(b_b_gates[0]).reshape(1, 2 * LANES),
             hn_g=b_hnorm_g[0].reshape(1, e), skip=b_skip[0].reshape(1, e),
             w_out=_col_slabs(b_w_out[0]))
    hand, conv_s = _mlstm_sample_front(xs1, w, state_mlstm_conv[0].transpose(1, 0, 2),
                                       seg=dec_t, ns_front=dec_b // 2)
    conv_s = conv_s.transpose(1, 0, 2)
    m0 = state_mlstm_m[0].reshape(dec_b, 1, HEADS)
    xp1, ys, c_s, n_s, m_s = _sgu_prompt_with_sample_recurrence(
        xp, (ng0, a_in, lng, lnb, a_w_s[0], a_b_s[0].T, a_out), xs1, hand,
        (state_mlstm_C[0], state_mlstm_n[0], m0), w, seg=dec_t)
    yp, c_p, n_p, m_p, conv_p = _mlstm_prompt(xp1, w, n_streams=bsz, seg=MXU_DIM)

    return (yp.reshape(bsz, seq, d),
            ys.reshape(dec_b, dec_t, d),
            v_s.reshape(1, dec_b, dec_t, e),
            c_p[None], n_p[None], m_p.reshape(1, bsz, HEADS), conv_p[None],
            c_s[None], n_s[None], m_s.reshape(1, dec_b, HEADS), conv_s[None])
```

```python
import functools
import math

import numpy as np
import jax
import jax.numpy as jnp
from jax import lax
from jax.experimental import pallas as pl
from jax.experimental.pallas import tpu as pltpu

D_MODEL = 1024
D_INNER = 2048
SGU_CHUNK = 128
SGU_GROUPS = 8
SGU_GROUP_DIM = D_INNER // SGU_GROUPS
HEADS = 4
HEAD_DIM = D_INNER // HEADS
QKV_BLOCK = 4
CONV_W = 4
RMS_EPS = 1e-6
LN_EPS = 1e-5

LANES = 128
SUBLANES = 8
MXU_DIM = 256
VMEM_LIMIT_BYTES = 60 * 1024 * 1024
SLAB_BLOCK_BYTES = 6 * 1024 * 1024

BF16 = jnp.bfloat16
F32 = jnp.float32
LOG2E = math.log2(math.e)

assert SGU_GROUP_DIM == MXU_DIM


def _dot(a, b):
    return jnp.dot(a, b, preferred_element_type=F32)


def _rmsnorm(x, g):
    return x * lax.rsqrt(jnp.mean(x * x, axis=-1, keepdims=True) + RMS_EPS) * g


def _log_sigmoid(x):
    return jnp.minimum(x, 0.0) - jnp.log1p(jnp.exp(-jnp.abs(x)))


def _prefix_rows(x, op, fill):
    n = x.shape[0]
    row = lax.broadcasted_iota(jnp.int32, x.shape, 0)
    shift = 1
    while shift < n:
        if shift % SUBLANES == 0:
            shifted = jnp.concatenate(
                [jnp.full((shift, x.shape[1]), fill, x.dtype), x[:n - shift]], axis=0)
        else:
            shifted = jnp.where(row >= shift, pltpu.roll(x, shift, 0), fill)
        x = op(x, shifted)
        shift *= 2
    return x


def _const_spec(shape):
    zeros = (0,) * len(shape)
    return pl.BlockSpec(shape, lambda *_: zeros, pipeline_mode=pl.Buffered(1))


def _slab_kernel(w_ref, o_ref):
    for j in range(o_ref.shape[0]):
        o_ref[j] = w_ref[:, j * MXU_DIM:(j + 1) * MXU_DIM].astype(BF16)


def _col_slabs(w):
    k, n = w.shape
    rows = MXU_DIM
    while 2 * rows <= k and 2 * rows * n * 4 <= SLAB_BLOCK_BYTES:
        rows *= 2
    return pl.pallas_call(
        _slab_kernel,
        grid=(k // rows,),
        in_specs=[pl.BlockSpec((rows, n), lambda i: (i, 0))],
        out_specs=pl.BlockSpec((n // MXU_DIM, rows, MXU_DIM), lambda i: (0, i, 0)),
        out_shape=jax.ShapeDtypeStruct((n // MXU_DIM, k, MXU_DIM), BF16),
        compiler_params=pltpu.CompilerParams(dimension_semantics=("arbitrary",)),
        name="weight_slabs",
    )(w)


def _sgu_kernel(x_ref, ng_ref, win_ref, lng_ref, lnb_ref, ws_ref, bs_ref, wout_ref,
                y_ref, *v_refs, tm):
    _interleave(_sgu_stages(x_ref, ng_ref, win_ref, lng_ref, lnb_ref, ws_ref, bs_ref, wout_ref,
                            y_ref, v_refs, tm=tm))


def _sgu_stages(x_ref, ng_ref, win_ref, lng_ref, lnb_ref, ws_ref, bs_ref, wout_ref,
                y_ref, v_refs, *, tm):
    ng = SGU_GROUPS
    e = D_INNER
    x = x_ref[...]
    xn = _rmsnorm(x, ng_ref[...]).astype(BF16)

    v_g, uz_g = [], []
    row_sum = jnp.zeros((tm, 1), F32)
    for g in range(ng):
        v = jax.nn.gelu(_dot(xn, win_ref[ng + g]))
        row_sum = row_sum + jnp.sum(v, axis=1, keepdims=True)
        v_g.append(v)
        if g % 4 == 3:
            yield
    mu = row_sum * (1.0 / e)
    sq_sum = jnp.zeros((tm, 1), F32)
    for g in range(ng):
        u = jax.nn.gelu(_dot(xn, win_ref[g]))
        z = _dot(xn, win_ref[2 * ng + g])
        uz_g.append(u * jax.nn.silu(z))
        v_g[g] = v_g[g] - mu
        sq_sum = sq_sum + jnp.sum(v_g[g] * v_g[g], axis=1, keepdims=True)
        if g % 4 == 3:
            yield
    rstd = lax.rsqrt(sq_sum * (1.0 / e) + LN_EPS)

    ti = lax.broadcasted_iota(jnp.int32, (SGU_CHUNK, SGU_CHUNK), 0)
    si = lax.broadcasted_iota(jnp.int32, (SGU_CHUNK, SGU_CHUNK), 1)
    causal = si <= ti
    s_g = []
    for g in range(ng):
        lo, hi = g * SGU_GROUP_DIM, (g + 1) * SGU_GROUP_DIM
        vn = v_g[g] * rstd * lng_ref[:, lo:hi] + lnb_ref[:, lo:hi]
        if v_refs:
            v_refs[0][:, lo:hi] = vn
        vb = vn.astype(BF16)
        wm = jnp.where(causal, ws_ref[g], 0.0).astype(BF16)
        bias = bs_ref[:, g:g + 1]
        s_g.append(jnp.concatenate(
            [_dot(wm, vb[c * SGU_CHUNK:(c + 1) * SGU_CHUNK]) + bias
             for c in range(tm // SGU_CHUNK)], axis=0))
    yield
    n_out = D_MODEL // MXU_DIM
    acc = [None] * n_out
    for g in range(ng):
        lo, hi = g * SGU_GROUP_DIM, (g + 1) * SGU_GROUP_DIM
        out = (uz_g[g] * s_g[g]).astype(BF16)
        for n in range(n_out):
            part = _dot(out, wout_ref[n, lo:hi, :])
            acc[n] = part if acc[n] is None else acc[n] + part
        if g == ng // 2 - 1:
            yield
    y_ref[...] = x + jnp.concatenate(acc, axis=1)


def _sgu_layer(x, norm_g, w_in, ln_g, ln_b, w_s, b_s, w_out, *, tm, want_v):
    rows = x.shape[0]
    e = D_INNER
    row_spec = pl.BlockSpec((tm, D_MODEL), lambda i: (i, 0))
    out_shape = [jax.ShapeDtypeStruct((rows, D_MODEL), F32)]
    out_specs = [row_spec]
    if want_v:
        out_shape.append(jax.ShapeDtypeStruct((rows, e), F32))
        out_specs.append(pl.BlockSpec((tm, e), lambda i: (i, 0)))
    res = pl.pallas_call(
        functools.partial(_sgu_kernel, tm=tm),
        grid=(rows // tm,),
        in_specs=[row_spec,
                  _const_spec((1, D_MODEL)),
                  _const_spec(w_in.shape),
                  _const_spec((1, e)),
                  _const_spec((1, e)),
                  _const_spec((SGU_GROUPS, SGU_CHUNK, SGU_CHUNK)),
                  _const_spec((SGU_CHUNK, SGU_GROUPS)),
                  _const_spec(w_out.shape)],
        out_specs=out_specs,
        out_shape=out_shape,
        compiler_params=pltpu.CompilerParams(
            dimension_semantics=("arbitrary",), vmem_limit_bytes=VMEM_LIMIT_BYTES),
        name="sgu_layer_v" if want_v else "sgu_layer",
    )(x, norm_g, w_in, ln_g, ln_b, w_s, b_s, w_out)
    return res


def _cols_to_rows(a, seg):
    if seg % LANES == 0:
        return a.T[0:SUBLANES, :]
    ti = lax.broadcasted_iota(jnp.int32, (seg, seg), 0)
    si = lax.broadcasted_iota(jnp.int32, (seg, seg), 1)
    return jnp.concatenate(
        [jnp.sum(jnp.where(ti == si, a[:, h:h + 1], 0.0), axis=0, keepdims=True)
         for h in range(HEADS)], axis=0)


_HANDOFF = (('xc', D_INNER, F32), ('sz', D_INNER, F32), ('qb', D_INNER, BF16),
            ('ks', D_INNER, F32), ('vb', D_INNER, BF16), ('gates', 2 * LANES, F32))
_HANDOFF_NAMES = tuple(name for name, _, _ in _HANDOFF)


def _interleave(*gens):
    live = list(gens)
    while live:
        for gen in list(live):
            try:
                next(gen)
            except StopIteration:
                live.remove(gen)


def _frontend(x_ref, ng_ref, win_ref, cw_ref, cb_ref, bdq_ref, bdk_ref, bdv_ref, wg_ref,
              bg_ref, xpad_ref, dst, *, ns, seg):
    e, dh = D_INNER, HEAD_DIM
    pad = SUBLANES
    rows = ns * seg
    tiles_per_head = dh // MXU_DIM
    xn = _rmsnorm(x_ref[...], ng_ref[...]).astype(BF16)

    def project(h):
        t0 = h * tiles_per_head
        xm_ = jnp.concatenate(
            [_dot(xn, win_ref[t0 + j]) for j in range(tiles_per_head)], axis=1)
        z_ = jnp.concatenate(
            [_dot(xn, win_ref[e // MXU_DIM + t0 + j]) for j in range(tiles_per_head)], axis=1)
        return xm_, z_

    projected = project(0)
    gates = bg_ref[...]
    yield
    for h in range(HEADS):
        c0, c1 = h * dh, (h + 1) * dh
        t0 = h * tiles_per_head
        xm, z = projected
        if h + 1 < HEADS:
            projected = project(h + 1)
        xpad_ref[:, pad:pad + seg, c0:c1] = xm.reshape(ns, seg, dh)
        pre = cb_ref[:, c0:c1] + cw_ref[CONV_W - 1:CONV_W, c0:c1] * xm
        for d in range(1, CONV_W):
            shifted = xpad_ref[:, pad - d:pad - d + seg, c0:c1].reshape(rows, dh)
            pre = pre + cw_ref[CONV_W - 1 - d:CONV_W - d, c0:c1] * shifted
        xc = jax.nn.silu(pre)
        xcb = xc.astype(BF16)
        xmb = xm.astype(BF16)

        def blockdiag(a, w_ref):
            return jnp.concatenate(
                [_dot(a[:, j * MXU_DIM:(j + 1) * MXU_DIM], w_ref[t0 + j])
                 for j in range(tiles_per_head)], axis=1)

        q = blockdiag(xcb, bdq_ref)
        k = blockdiag(xcb, bdk_ref)
        v = blockdiag(xmb, bdv_ref)
        qb, vb = q.astype(BF16), v.astype(BF16)
        gates = gates + _dot(xcb, wg_ref[0, c0:c1, :]) + _dot(xmb, wg_ref[1, c0:c1, :])
        dst['xc'][:, c0:c1] = xc
        dst['sz'][:, c0:c1] = jax.nn.silu(z)
        dst['qb'][:, c0:c1] = qb
        dst['ks'][:, c0:c1] = k * (dh ** -0.5)
        dst['vb'][:, c0:c1] = vb
        if h == HEADS - 1:
            dst['gates'][...] = gates
        yield


def _backend(xres_ref, src, cin_ref, nin_ref, min_ref, hng_ref, skip_ref, wout_ref, fg_ref,
             y_ref, c_ref, n_ref, m_ref, *, ns, seg, memory_read_first=False):
    dh = HEAD_DIM
    n_out = D_MODEL // MXU_DIM
    ti = lax.broadcasted_iota(jnp.int32, (seg, seg), 0)
    si = lax.broadcasted_iota(jnp.int32, (seg, seg), 1)
    causal = si <= ti
    pairs = [(s, h) for s in range(ns) for h in range(HEADS)]

    def block(name, s, h):
        return src[name][s * seg:(s + 1) * seg, h * dh:(h + 1) * dh]

    qc, cprev = {}, {}
    if memory_read_first:
        qc = {(s, h): _dot(block('qb', s, h), cin_ref[s, h].astype(BF16)) for s, h in pairs}
    gt = []
    for s in range(ns):
        g = src['gates'][s * seg:(s + 1) * seg, :]
        ig = g[:, 0:LANES]
        lf = _log_sigmoid(g[:, LANES:2 * LANES])
        b = _prefix_rows(lf, jnp.add, 0.0)
        a = ig - b
        cm = _prefix_rows(a, jnp.maximum, -jnp.inf)
        m_prev = jnp.concatenate(
            [min_ref[s], jnp.zeros((1, LANES - HEADS), F32)], axis=1)
        neg_mt_plus_b = -jnp.maximum(m_prev, cm)
        b_last = b[seg - 1:seg, :]
        m_new = b_last + jnp.maximum(m_prev, cm[seg - 1:seg, :])
        gt.append(dict(
            w_inter=jnp.exp(m_prev + neg_mt_plus_b),
            inv_floor=jnp.exp(neg_mt_plus_b - b),
            w=jnp.exp(b_last + a - m_new),
            decay=jnp.exp(b_last + m_prev - m_new),
            a_rows=_cols_to_rows(a, seg) * LOG2E,
            u=neg_mt_plus_b * LOG2E,
            n_prev=nin_ref[s]))
        m_ref[s] = m_new[:, 0:HEADS]
    yield

    sc = {}
    for s, h in pairs:
        w_intra = jnp.where(
            causal, jnp.exp2(gt[s]['u'][:, h:h + 1] + gt[s]['a_rows'][h:h + 1, :]), 0.0)
        sc[s, h] = lax.dot_general(block('qb', s, h), block('ks', s, h).astype(BF16),
                                   (((1,), (1,)), ((), ())),
                                   preferred_element_type=F32) * w_intra
        if not memory_read_first:
            cprev[s, h] = cin_ref[s, h]
            qc[s, h] = _dot(block('qb', s, h), cprev[s, h].astype(BF16))
    yield

    num = {(s, h): _dot(sc[s, h].astype(BF16), block('vb', s, h))
           + gt[s]['w_inter'][:, h:h + 1] * qc[s, h] for s, h in pairs}
    yield

    out, n_new = {}, {}
    for s, h in pairs:
        c0, c1 = h * dh, (h + 1) * dh
        n_prev = gt[s]['n_prev'][h:h + 1, :]
        qn = jnp.sum(block('qb', s, h).astype(F32) * n_prev, axis=1, keepdims=True)
        den = jnp.sum(sc[s, h], axis=1, keepdims=True) + gt[s]['w_inter'][:, h:h + 1] * qn
        r = 1.0 / jnp.maximum(jnp.abs(den), gt[s]['inv_floor'][:, h:h + 1])
        cen = num[s, h] - jnp.mean(num[s, h], axis=1, keepdims=True)
        var = jnp.mean(cen * cen, axis=1, keepdims=True)
        hn = cen * (r * lax.rsqrt(r * r * var + LN_EPS)) * hng_ref[:, c0:c1]
        out[s, h] = ((hn + skip_ref[:, c0:c1] * block('xc', s, h))
                     * block('sz', s, h)).astype(BF16)

        decay = gt[s]['decay'][:, h:h + 1]
        kw = block('ks', s, h) * gt[s]['w'][:, h:h + 1]
        upd = lax.dot_general(kw.astype(BF16), block('vb', s, h), (((0,), (0,)), ((), ())),
                              preferred_element_type=F32)
        c_old = cin_ref[s, h] if memory_read_first else cprev[s, h]
        c_ref[s, h] = decay * c_old + upd
        n_new[s, h] = decay * n_prev + jnp.sum(kw, axis=0, keepdims=True)
    for s in range(ns):
        n_ref[s] = jnp.concatenate([n_new[s, h] for h in range(HEADS)], axis=0)
    yield

    y_rows = []
    for s in range(ns):
        acc = [None] * n_out
        for h in range(HEADS):
            for n in range(n_out):
                part = _dot(out[s, h], wout_ref[n, h * dh:(h + 1) * dh, :])
                acc[n] = part if acc[n] is None else acc[n] + part
        y_rows.append(jnp.concatenate(acc, axis=1))
    proj = y_rows[0] if ns == 1 else jnp.concatenate(y_rows, axis=0)
    y_ref[...] = _rmsnorm(xres_ref[...] + proj, fg_ref[...])


def _mlstm_pipelined_kernel(x_ref, xprev_ref, ng_ref, fg_ref, win_ref, cw_ref, cb_ref,
                            bd_ref, wg_ref, bg_ref, hng_ref, skip_ref,
                            wout_ref, y_ref, c_ref, n_ref, m_ref, conv_ref, xpad_ref,
                            *handoff, seg, n_tiles, total):
    t = pl.program_id(0)
    e = D_INNER
    pad = SUBLANES
    n_hand = len(_HANDOFF)
    sets = (dict(zip(_HANDOFF_NAMES, handoff[:n_hand])),
            dict(zip(_HANDOFF_NAMES, handoff[n_hand:])))

    @pl.when(t % n_tiles == 0)
    def _():
        xpad_ref[:, 0:pad, :] = jnp.zeros((1, pad, e), F32)

    @pl.when((t - 1) % n_tiles == 0)
    def _():
        c_ref[...] = jnp.zeros(c_ref.shape, F32)
        n_ref[...] = jnp.zeros(n_ref.shape, F32)
        m_ref[...] = jnp.zeros(m_ref.shape, F32)

    def step(dst, src):
        gens = []
        if src is not None:
            gens.append(_backend(xprev_ref, src, c_ref, n_ref, m_ref, hng_ref, skip_ref,
                                 wout_ref, fg_ref, y_ref, c_ref, n_ref, m_ref, ns=1, seg=seg))
        if dst is not None:
            gens.append(_frontend(x_ref, ng_ref, win_ref, cw_ref, cb_ref, bd_ref.at[0],
                                  bd_ref.at[1], bd_ref.at[2], wg_ref, bg_ref, xpad_ref, dst,
                                  ns=1, seg=seg))
        _interleave(*reversed(gens))
        if dst is not None:
            conv_ref[...] = xpad_ref[:, pad + seg - (CONV_W - 1):pad + seg, :]
            xpad_ref[:, 0:pad, :] = xpad_ref[:, seg:seg + pad, :]

    @pl.when(t == 0)
    def _():
        step(sets[0], None)

    @pl.when(t == total)
    def _():
        step(None, sets[(total - 1) % 2])

    inner = (t > 0) & (t < total)

    @pl.when(inner & (t % 2 == 0))
    def _():
        step(sets[0], sets[1])

    @pl.when(inner & (t % 2 == 1))
    def _():
        step(sets[1], sets[0])


def _mlstm_front_kernel(x_ref, ng_ref, win_ref, cw_ref, cb_ref, bd_ref, wg_ref, bg_ref,
                        conv0_ref, *rest, ns, seg):
    n_hand = len(_HANDOFF)
    dst = dict(zip(_HANDOFF_NAMES, rest[:n_hand]))
    conv_ref, xpad_ref = rest[n_hand:]
    pad = SUBLANES
    for j in range(CONV_W - 1):
        xpad_ref[:, pad - (CONV_W - 1) + j, :] = conv0_ref[j]
    _interleave(_frontend(x_ref, ng_ref, win_ref, cw_ref, cb_ref, bd_ref.at[0], bd_ref.at[1],
                          bd_ref.at[2], wg_ref, bg_ref, xpad_ref, dst, ns=ns, seg=seg))
    for j in range(CONV_W - 1):
        conv_ref[j] = xpad_ref[:, pad + seg - (CONV_W - 1) + j, :]


def _sgu_recurrence_kernel(*refs, tm, seg):
    n_hand = len(_HANDOFF)
    sgu_refs, rest = refs[:8], refs[8:]
    xres_ref = rest[0]
    src = dict(zip(_HANDOFF_NAMES, rest[1:1 + n_hand]))
    (c0_ref, n0_ref, m0_ref, hng_ref, skip_ref, wout_ref, fg_ref,
     y_ref, ys_ref, c_ref, n_ref, m_ref) = rest[1 + n_hand:]
    _interleave(
        _backend(xres_ref, src, c0_ref, n0_ref, m0_ref, hng_ref, skip_ref, wout_ref,
                 fg_ref, ys_ref, c_ref, n_ref, m_ref, ns=1, seg=seg, memory_read_first=True),
        _sgu_stages(*sgu_refs, y_ref, (), tm=tm))


def _state_specs(ns, index):
    dh = HEAD_DIM
    return [pl.BlockSpec((ns, HEADS, dh, dh), lambda *i: (index(*i), 0, 0, 0)),
            pl.BlockSpec((ns, HEADS, dh), lambda *i: (index(*i), 0, 0)),
            pl.BlockSpec((ns, 1, HEADS), lambda *i: (index(*i), 0, 0))]


def _state_shapes(n_streams):
    dh = HEAD_DIM
    return [jax.ShapeDtypeStruct((n_streams, HEADS, dh, dh), F32),
            jax.ShapeDtypeStruct((n_streams, HEADS, dh), F32),
            jax.ShapeDtypeStruct((n_streams, 1, HEADS), F32)]


def _front_weight_args(w):
    return [w['norm_g'], w['w_in'], w['conv_w'], w['conv_b'], w['bd'], w['wg'], w['bg']]


def _front_weight_specs(w):
    return [_const_spec(a.shape) for a in _front_weight_args(w)]


def _mlstm_prompt(x, w, *, n_streams, seg):
    e = D_INNER
    n_tiles = x.shape[0] // (n_streams * seg)
    total = n_streams * n_tiles

    def front_tile(t):
        return jnp.minimum(t, total - 1)

    def back_tile(t):
        return jnp.maximum(t - 1, 0)

    cur_spec = pl.BlockSpec((seg, D_MODEL), lambda t: (front_tile(t), 0))
    prev_spec = pl.BlockSpec((seg, D_MODEL), lambda t: (back_tile(t), 0))
    conv_spec = pl.BlockSpec((1, CONV_W - 1, e), lambda t: (front_tile(t) // n_tiles, 0, 0))
    back_specs = [_const_spec((1, e)), _const_spec((1, e)), _const_spec(w['w_out'].shape)]
    fw = _front_weight_specs(w)
    in_specs = [cur_spec, prev_spec, fw[0], _const_spec((1, D_MODEL))] + fw[1:] + back_specs
    fa = _front_weight_args(w)
    args = [x, x, fa[0], w['final_g']] + fa[1:] + [w['hn_g'], w['skip'], w['w_out']]
    handoff = [pltpu.VMEM((seg, width), dtype)
               for _ in range(2) for _, width, dtype in _HANDOFF]
    return pl.pallas_call(
        functools.partial(_mlstm_pipelined_kernel, seg=seg, n_tiles=n_tiles, total=total),
        grid=(total + 1,),
        in_specs=in_specs,
        out_specs=([prev_spec] + _state_specs(1, lambda t: back_tile(t) // n_tiles)
                   + [conv_spec]),
        out_shape=([jax.ShapeDtypeStruct(x.shape, F32)] + _state_shapes(n_streams)
                   + [jax.ShapeDtypeStruct((n_streams, CONV_W - 1, e), F32)]),
        scratch_shapes=[pltpu.VMEM((1, SUBLANES + seg, e), F32)] + handoff,
        compiler_params=pltpu.CompilerParams(
            dimension_semantics=("arbitrary",),
            vmem_limit_bytes=VMEM_LIMIT_BYTES),
        name="mlstm_prompt",
    )(*args)


def _mlstm_sample_front(x, w, conv0, *, seg, ns_front):
    e = D_INNER
    n_streams = conv0.shape[1]
    rows_f = ns_front * seg
    hand_shapes = [jax.ShapeDtypeStruct((x.shape[0], width), dtype)
                   for _, width, dtype in _HANDOFF]
    conv_spec = pl.BlockSpec((CONV_W - 1, ns_front, e), lambda i: (0, i, 0))
    res = pl.pallas_call(
        functools.partial(_mlstm_front_kernel, ns=ns_front, seg=seg),
        grid=(n_streams // ns_front,),
        in_specs=([pl.BlockSpec((rows_f, D_MODEL), lambda i: (i, 0))]
                  + _front_weight_specs(w) + [conv_spec]),
        out_specs=([pl.BlockSpec((rows_f, width), lambda i: (i, 0)) for _, width, _ in _HANDOFF]
                   + [conv_spec]),
        out_shape=hand_shapes + [jax.ShapeDtypeStruct((CONV_W - 1, n_streams, e), F32)],
        scratch_shapes=[pltpu.VMEM((ns_front, SUBLANES + seg, e), F32)],
        compiler_params=pltpu.CompilerParams(
            dimension_semantics=("arbitrary",), vmem_limit_bytes=VMEM_LIMIT_BYTES),
        name="mlstm_sample_front",
    )(x, *_front_weight_args(w), conv0)
    return res[:-1], res[-1]


def _sgu_prompt_with_sample_recurrence(xp, sgu_args, xs, hand, state, w, *, seg):
    e = D_INNER
    c0, n0, m0 = state
    n_streams = c0.shape[0]
    assert xp.shape[0] % n_streams == 0 and (xp.shape[0] // n_streams) % SGU_CHUNK == 0
    tm = xp.shape[0] // n_streams
    norm_g, w_in, ln_g, ln_b, w_s, b_s, w_out = sgu_args

    sgu_specs = [pl.BlockSpec((tm, D_MODEL), lambda i: (i, 0)),
                 _const_spec((1, D_MODEL)), _const_spec(w_in.shape),
                 _const_spec((1, e)), _const_spec((1, e)),
                 _const_spec((SGU_GROUPS, SGU_CHUNK, SGU_CHUNK)),
                 _const_spec((SGU_CHUNK, SGU_GROUPS)), _const_spec(w_out.shape)]
    hand_specs = [pl.BlockSpec((seg, width), lambda i: (i, 0)) for _, width, _ in _HANDOFF]
    state_specs = _state_specs(1, lambda i: i)
    ys_spec = pl.BlockSpec((seg, D_MODEL), lambda i: (i, 0))
    xp1, ys, c_out, n_out, m_out = pl.pallas_call(
        functools.partial(_sgu_recurrence_kernel, tm=tm, seg=seg),
        grid=(n_streams,),
        in_specs=(sgu_specs + [ys_spec] + hand_specs + state_specs
                  + [_const_spec((1, e)), _const_spec((1, e)), _const_spec(w['w_out'].shape),
                     _const_spec((1, D_MODEL))]),
        out_specs=[pl.BlockSpec((tm, D_MODEL), lambda i: (i, 0)), ys_spec] + state_specs,
        out_shape=([jax.ShapeDtypeStruct(xp.shape, F32), jax.ShapeDtypeStruct(xs.shape, F32)]
                   + _state_shapes(n_streams)),
        compiler_params=pltpu.CompilerParams(
            dimension_semantics=("arbitrary",), vmem_limit_bytes=VMEM_LIMIT_BYTES),
        name="sgu_prompt_sample_recurrence",
    )(xp, norm_g, w_in, ln_g, ln_b, w_s, b_s, w_out, xs, *hand, c0, n0, m0,
      w['hn_g'], w['skip'], w['w_out'], w['final_g'])
    return xp1, ys, c_out, n_out, m_out


def _qkv_weights_kernel(rows_ref, wg_ref, bd_ref, fold_ref):
    nt = rows_ref.shape[1]
    r_blk = lax.broadcasted_iota(jnp.int32, (MXU_DIM, MXU_DIM), 0) // QKV_BLOCK
    c_idx = lax.broadcasted_iota(jnp.int32, (MXU_DIM, MXU_DIM), 1)
    same_block = r_blk == c_idx // QKV_BLOCK
    for j in range(nt):
        r0, r1 = j * MXU_DIM, (j + 1) * MXU_DIM
        tiles = []
        for a in range(3):
            rows = rows_ref[a, j]
            tiled = jnp.zeros((MXU_DIM, MXU_DIM), F32)
            for o in range(QKV_BLOCK):
                tiled = jnp.where(c_idx % QKV_BLOCK == o, rows[:, o:o + 1], tiled)
            tile = jnp.where(same_block, tiled, 0.0).astype(BF16)
            bd_ref[a, j] = tile
            tiles.append(tile)
        fold_ref[0, r0:r1, :] = (_dot(tiles[0], wg_ref[0, r0:r1, :])
                                 + _dot(tiles[1], wg_ref[1, r0:r1, :])).astype(BF16)
        fold_ref[1, r0:r1, :] = _dot(tiles[2], wg_ref[2, r0:r1, :]).astype(BF16)


def _qkv_weights(wq, wk, wv, wg):
    nt = D_INNER // MXU_DIM
    rows = jnp.stack([wq, wk, wv]).reshape(3, nt, MXU_DIM, QKV_BLOCK)
    return pl.pallas_call(
        _qkv_weights_kernel,
        out_shape=[jax.ShapeDtypeStruct((3, nt, MXU_DIM, MXU_DIM), BF16),
                   jax.ShapeDtypeStruct((2, D_INNER, wg.shape[-1]), BF16)],
        name="qkv_weights",
    )(rows, wg)


def _head_lanes(w):
    lead = w.shape[:-1]
    z = jnp.zeros(lead + (LANES - HEADS,), w.dtype)
    return jnp.concatenate([w[..., :HEADS], z, w[..., HEADS:], z], axis=-1)


def kernel(x_prompt, x_sample, state_mlstm_C, state_mlstm_n, state_mlstm_m, state_mlstm_conv, norm_g, final_norm_g, a_w_in, a_ln_g, a_ln_b, a_w_s, a_b_s, a_w_out, b_w_in, b_conv_w, b_conv_b, b_wq, b_wk, b_wv, b_w_gates, b_b_gates, b_hnorm_g, b_skip, b_w_out):
    bsz, seq, d = x_prompt.shape
    dec_b, dec_t, _ = x_sample.shape
    e = D_INNER
    assert norm_g.shape[0] == 2 and a_w_in.shape[0] == 1 and b_w_in.shape[0] == 1
    assert d == D_MODEL and seq % MXU_DIM == 0 and SGU_CHUNK % dec_t == 0

    xp = x_prompt.reshape(bsz * seq, d)
    xs = x_sample.reshape(dec_b * dec_t, d)

    a_in = _col_slabs(a_w_in[0])
    a_out = _col_slabs(a_w_out[0])
    ng0 = norm_g[0].reshape(1, d)
    lng = a_ln_g[0].reshape(1, e)
    lnb = a_ln_b[0].reshape(1, e)
    per_tile = SGU_CHUNK // dec_t
    w_head = a_w_s[0][:, :dec_t, :dec_t]
    ws_s = jnp.einsum('ab,gts->gatbs', np.eye(per_tile, dtype=np.float32), w_head)
    ws_s = ws_s.reshape(SGU_GROUPS, SGU_CHUNK, SGU_CHUNK)
    bs_s = jnp.tile(a_b_s[0][:, :dec_t], (1, per_tile)).T
    xs1, v_s = _sgu_layer(xs, ng0, a_in, lng, lnb, ws_s, bs_s, a_out,
                          tm=2 * SGU_CHUNK, want_v=True)

    wg = _head_lanes(b_w_gates[0]).astype(BF16).reshape(3, e, 2 * LANES)
    bd, wg_folded = _qkv_weights(b_wq[0], b_wk[0], b_wv[0], wg)
    w = dict(norm_g=norm_g[1].reshape(1, d), final_g=final_norm_g.reshape(1, d),
             w_in=_col_slabs(b_w_in[0]), conv_w=b_conv_w[0], conv_b=b_conv_b[0].reshape(1, e),
             bd=bd, wg=wg_folded,
             bg=_head_lanes(b_b_gates[0]).reshape(1, 2 * LANES),
             hn_g=b_hnorm_g[0].reshape(1, e), skip=b_skip[0].reshape(1, e),
             w_out=_col_slabs(b_w_out[0]))
    hand, conv_s = _mlstm_sample_front(xs1, w, state_mlstm_conv[0].transpose(1, 0, 2),
                                       seg=dec_t, ns_front=dec_b // 2)
    conv_s = conv_s.transpose(1, 0, 2)
    m0 = state_mlstm_m[0].reshape(dec_b, 1, HEADS)
    xp1, ys, c_s, n_s, m_s = _sgu_prompt_with_sample_recurrence(
        xp, (ng0, a_in, lng, lnb, a_w_s[0], a_b_s[0].T, a_out), xs1, hand,
        (state_mlstm_C[0], state_mlstm_n[0], m0), w, seg=dec_t)
    yp, c_p, n_p, m_p, conv_p = _mlstm_prompt(xp1, w, n_streams=bsz, seg=MXU_DIM)

    return (yp.reshape(bsz, seq, d),
            ys.reshape(dec_b, dec_t, d),
            v_s.reshape(1, dec_b, dec_t, e),
            c_p[None], n_p[None], m_p.reshape(1, bsz, HEADS), conv_p[None],
            c_s[None], n_s[None], m_s.reshape(1, dec_b, HEADS), conv_s[None])
```

```python
import functools
import math

import numpy as np
import jax
import jax.numpy as jnp
from jax import lax
from jax.experimental import pallas as pl
from jax.experimental.pallas import tpu as pltpu

D_MODEL = 1024
D_INNER = 2048
SGU_CHUNK = 128
SGU_GROUPS = 8
SGU_GROUP_DIM = D_INNER // SGU_GROUPS
HEADS = 4
HEAD_DIM = D_INNER // HEADS
QKV_BLOCK = 4
CONV_W = 4
RMS_EPS = 1e-6
LN_EPS = 1e-5

LANES = 128
SUBLANES = 8
MXU_DIM = 256
VMEM_LIMIT_BYTES = 60 * 1024 * 1024
SLAB_BLOCK_BYTES = 6 * 1024 * 1024

BF16 = jnp.bfloat16
F32 = jnp.float32
LOG2E = math.log2(math.e)

assert SGU_GROUP_DIM == MXU_DIM


def _dot(a, b):
    return jnp.dot(a, b, preferred_element_type=F32)


def _rmsnorm(x, g):
    return x * lax.rsqrt(jnp.mean(x * x, axis=-1, keepdims=True) + RMS_EPS) * g


def _log_sigmoid(x):
    return jnp.minimum(x, 0.0) - jnp.log1p(jnp.exp(-jnp.abs(x)))


def _prefix_rows(x, op, fill):
    n = x.shape[0]
    row = lax.broadcasted_iota(jnp.int32, x.shape, 0)
    shift = 1
    while shift < n:
        if shift % SUBLANES == 0:
            shifted = jnp.concatenate(
                [jnp.full((shift, x.shape[1]), fill, x.dtype), x[:n - shift]], axis=0)
        else:
            shifted = jnp.where(row >= shift, pltpu.roll(x, shift, 0), fill)
        x = op(x, shifted)
        shift *= 2
    return x


def _const_spec(shape):
    zeros = (0,) * len(shape)
    return pl.BlockSpec(shape, lambda *_: zeros, pipeline_mode=pl.Buffered(1))


def _slab_kernel(w_ref, o_ref):
    for j in range(o_ref.shape[0]):
        o_ref[j] = w_ref[:, j * MXU_DIM:(j + 1) * MXU_DIM].astype(BF16)


def _col_slabs(w):
    k, n = w.shape
    rows = MXU_DIM
    while 2 * rows <= k and 2 * rows * n * 4 <= SLAB_BLOCK_BYTES:
        rows *= 2
    return pl.pallas_call(
        _slab_kernel,
        grid=(k // rows,),
        in_specs=[pl.BlockSpec((rows, n), lambda i: (i, 0))],
        out_specs=pl.BlockSpec((n // MXU_DIM, rows, MXU_DIM), lambda i: (0, i, 0)),
        out_shape=jax.ShapeDtypeStruct((n // MXU_DIM, k, MXU_DIM), BF16),
        compiler_params=pltpu.CompilerParams(dimension_semantics=("arbitrary",)),
        name="weight_slabs",
    )(w)


def _sgu_kernel(x_ref, ng_ref, win_ref, lng_ref, lnb_ref, ws_ref, bs_ref, wout_ref,
                y_ref, *v_refs, tm):
    _interleave(_sgu_stages(x_ref, ng_ref, win_ref, lng_ref, lnb_ref, ws_ref, bs_ref, wout_ref,
                            y_ref, v_refs, tm=tm))


def _sgu_stages(x_ref, ng_ref, win_ref, lng_ref, lnb_ref, ws_ref, bs_ref, wout_ref,
                y_ref, v_refs, *, tm):
    ng = SGU_GROUPS
    e = D_INNER
    x = x_ref[...]
    xn = _rmsnorm(x, ng_ref[...]).astype(BF16)

    v_g, uz_g = [], []
    row_sum = jnp.zeros((tm, 1), F32)
    for g in range(ng):
        v = jax.nn.gelu(_dot(xn, win_ref[ng + g]))
        row_sum = row_sum + jnp.sum(v, axis=1, keepdims=True)
        v_g.append(v)
        if g % 4 == 3:
            yield
    mu = row_sum * (1.0 / e)
    sq_sum = jnp.zeros((tm, 1), F32)
    for g in range(ng):
        u = jax.nn.gelu(_dot(xn, win_ref[g]))
        z = _dot(xn, win_ref[2 * ng + g])
        uz_g.append(u * jax.nn.silu(z))
        v_g[g] = v_g[g] - mu
        sq_sum = sq_sum + jnp.sum(v_g[g] * v_g[g], axis=1, keepdims=True)
        if g % 4 == 3:
            yield
    rstd = lax.rsqrt(sq_sum * (1.0 / e) + LN_EPS)

    ti = lax.broadcasted_iota(jnp.int32, (SGU_CHUNK, SGU_CHUNK), 0)
    si = lax.broadcasted_iota(jnp.int32, (SGU_CHUNK, SGU_CHUNK), 1)
    causal = si <= ti
    s_g = []
    for g in range(ng):
        lo, hi = g * SGU_GROUP_DIM, (g + 1) * SGU_GROUP_DIM
        vn = v_g[g] * rstd * lng_ref[:, lo:hi] + lnb_ref[:, lo:hi]
        if v_refs:
            v_refs[0][:, lo:hi] = vn
        vb = vn.astype(BF16)
        wm = jnp.where(causal, ws_ref[g], 0.0).astype(BF16)
        bias = bs_ref[:, g:g + 1]
        s_g.append(jnp.concatenate(
            [_dot(wm, vb[c * SGU_CHUNK:(c + 1) * SGU_CHUNK]) + bias
             for c in range(tm // SGU_CHUNK)], axis=0))
    yield
    n_out = D_MODEL // MXU_DIM
    acc = [None] * n_out
    for g in range(ng):
        lo, hi = g * SGU_GROUP_DIM, (g + 1) * SGU_GROUP_DIM
        out = (uz_g[g] * s_g[g]).astype(BF16)
        for n in range(n_out):
            part = _dot(out, wout_ref[n, lo:hi, :])
            acc[n] = part if acc[n] is None else acc[n] + part
        if g == ng // 2 - 1:
            yield
    y_ref[...] = x + jnp.concatenate(acc, axis=1)


def _sgu_layer(x, norm_g, w_in, ln_g, ln_b, w_s, b_s, w_out, *, tm, want_v):
    rows = x.shape[0]
    e = D_INNER
    row_spec = pl.BlockSpec((tm, D_MODEL), lambda i: (i, 0))
    out_shape = [jax.ShapeDtypeStruct((rows, D_MODEL), F32)]
    out_specs = [row_spec]
    if want_v:
        out_shape.append(jax.ShapeDtypeStruct((rows, e), F32))
        out_specs.append(pl.BlockSpec((tm, e), lambda i: (i, 0)))
    res = pl.pallas_call(
        functools.partial(_sgu_kernel, tm=tm),
        grid=(rows // tm,),
        in_specs=[row_spec,
                  _const_spec((1, D_MODEL)),
                  _const_spec(w_in.shape),
                  _const_spec((1, e)),
                  _const_spec((1, e)),
                  _const_spec((SGU_GROUPS, SGU_CHUNK, SGU_CHUNK)),
                  _const_spec((SGU_CHUNK, SGU_GROUPS)),
                  _const_spec(w_out.shape)],
        out_specs=out_specs,
        out_shape=out_shape,
        compiler_params=pltpu.CompilerParams(
            dimension_semantics=("arbitrary",), vmem_limit_bytes=VMEM_LIMIT_BYTES),
        name="sgu_layer_v" if want_v else "sgu_layer",
    )(x, norm_g, w_in, ln_g, ln_b, w_s, b_s, w_out)
    return res


def _cols_to_rows(a, seg):
    if seg % LANES == 0:
        return a.T[0:SUBLANES, :]
    ti = lax.broadcasted_iota(jnp.int32, (seg, seg), 0)
    si = lax.broadcasted_iota(jnp.int32, (seg, seg), 1)
    return jnp.concatenate(
        [jnp.sum(jnp.where(ti == si, a[:, h:h + 1], 0.0), axis=0, keepdims=True)
         for h in range(HEADS)], axis=0)


_HANDOFF = (('xc', D_INNER, F32), ('sz', D_INNER, F32), ('qb', D_INNER, BF16),
            ('ks', D_INNER, F32), ('vb', D_INNER, BF16), ('gates', 2 * LANES, F32))
_HANDOFF_NAMES = tuple(name for name, _, _ in _HANDOFF)


def _interleave(*gens):
    live = list(gens)
    while live:
        for gen in list(live):
            try:
                next(gen)
            except StopIteration:
                live.remove(gen)


def _frontend(x_ref, ng_ref, win_ref, cw_ref, cb_ref, bdq_ref, bdk_ref, bdv_ref, wg_ref,
              bg_ref, xpad_ref, dst, *, ns, seg):
    e, dh = D_INNER, HEAD_DIM
    pad = SUBLANES
    rows = ns * seg
    tiles_per_head = dh // MXU_DIM
    xn = _rmsnorm(x_ref[...], ng_ref[...]).astype(BF16)

    def project(h):
        t0 = h * tiles_per_head
        xm_ = jnp.concatenate(
            [_dot(xn, win_ref[t0 + j]) for j in range(tiles_per_head)], axis=1)
        z_ = jnp.concatenate(
            [_dot(xn, win_ref[e // MXU_DIM + t0 + j]) for j in range(tiles_per_head)], axis=1)
        return xm_, z_

    projected = project(0)
    gates = bg_ref[...]
    yield
    for h in range(HEADS):
        c0, c1 = h * dh, (h + 1) * dh
        t0 = h * tiles_per_head
        xm, z = projected
        if h + 1 < HEADS:
            projected = project(h + 1)
        xpad_ref[:, pad:pad + seg, c0:c1] = xm.reshape(ns, seg, dh)
        pre = cb_ref[:, c0:c1] + cw_ref[CONV_W - 1:CONV_W, c0:c1] * xm
        for d in range(1, CONV_W):
            shifted = xpad_ref[:, pad - d:pad - d + seg, c0:c1].reshape(rows, dh)
            pre = pre + cw_ref[CONV_W - 1 - d:CONV_W - d, c0:c1] * shifted
        xc = jax.nn.silu(pre)
        xcb = xc.astype(BF16)
        xmb = xm.astype(BF16)

        def blockdiag(a, w_ref):
            return jnp.concatenate(
                [_dot(a[:, j * MXU_DIM:(j + 1) * MXU_DIM], w_ref[t0 + j])
                 for j in range(tiles_per_head)], axis=1)

        q = blockdiag(xcb, bdq_ref)
        k = blockdiag(xcb, bdk_ref)
        v = blockdiag(xmb, bdv_ref)
        qb, vb = q.astype(BF16), v.astype(BF16)
        gates = gates + _dot(xcb, wg_ref[0, c0:c1, :]) + _dot(xmb, wg_ref[1, c0:c1, :])
        dst['xc'][:, c0:c1] = xc
        dst['sz'][:, c0:c1] = jax.nn.silu(z)
        dst['qb'][:, c0:c1] = qb
        dst['ks'][:, c0:c1] = k * (dh ** -0.5)
        dst['vb'][:, c0:c1] = vb
        if h == HEADS - 1:
            dst['gates'][...] = gates
        yield


def _backend(xres_ref, src, cin_ref, nin_ref, min_ref, hng_ref, skip_ref, wout_ref, fg_ref,
             y_ref, c_ref, n_ref, m_ref, *, ns, seg, memory_read_first=False):
    dh = HEAD_DIM
    n_out = D_MODEL // MXU_DIM
    ti = lax.broadcasted_iota(jnp.int32, (seg, seg), 0)
    si = lax.broadcasted_iota(jnp.int32, (seg, seg), 1)
    causal = si <= ti
    pairs = [(s, h) for s in range(ns) for h in range(HEADS)]

    def block(name, s, h):
        return src[name][s * seg:(s + 1) * seg, h * dh:(h + 1) * dh]

    qc = {}
    if memory_read_first:
        qc = {(s, h): _dot(block('qb', s, h), cin_ref[s, h].astype(BF16)) for s, h in pairs}
    gt = []
    for s in range(ns):
        g = src['gates'][s * seg:(s + 1) * seg, :]
        ig = g[:, 0:LANES]
        lf = _log_sigmoid(g[:, LANES:2 * LANES])
        b = _prefix_rows(lf, jnp.add, 0.0)
        a = ig - b
        cm = _prefix_rows(a, jnp.maximum, -jnp.inf)
        m_prev = jnp.concatenate(
            [min_ref[s], jnp.zeros((1, LANES - HEADS), F32)], axis=1)
        neg_mt_plus_b = -jnp.maximum(m_prev, cm)
        b_last = b[seg - 1:seg, :]
        m_new = b_last + jnp.maximum(m_prev, cm[seg - 1:seg, :])
        gt.append(dict(
            w_inter=jnp.exp(m_prev + neg_mt_plus_b),
            inv_floor=jnp.exp(neg_mt_plus_b - b),
            w=jnp.exp(b_last + a - m_new),
            decay=jnp.exp(b_last + m_prev - m_new),
            a_rows=_cols_to_rows(a, seg) * LOG2E,
            u=neg_mt_plus_b * LOG2E,
            n_prev=nin_ref[s]))
        m_ref[s] = m_new[:, 0:HEADS]
    yield

    sc = {}
    for s, h in pairs:
        w_intra = jnp.where(
            causal, jnp.exp2(gt[s]['u'][:, h:h + 1] + gt[s]['a_rows'][h:h + 1, :]), 0.0)
        sc[s, h] = lax.dot_general(block('qb', s, h), block('ks', s, h).astype(BF16),
                                   (((1,), (1,)), ((), ())),
                                   preferred_element_type=F32) * w_intra
        if not memory_read_first:
            qc[s, h] = _dot(block('qb', s, h), cin_ref[s, h].astype(BF16))
    yield

    num = {(s, h): _dot(sc[s, h].astype(BF16), block('vb', s, h))
           + gt[s]['w_inter'][:, h:h + 1] * qc[s, h] for s, h in pairs}
    yield

    out, n_new = {}, {}
    for s, h in pairs:
        c0, c1 = h * dh, (h + 1) * dh
        n_prev = gt[s]['n_prev'][h:h + 1, :]
        qn = jnp.sum(block('qb', s, h).astype(F32) * n_prev, axis=1, keepdims=True)
        den = jnp.sum(sc[s, h], axis=1, keepdims=True) + gt[s]['w_inter'][:, h:h + 1] * qn
        r = 1.0 / jnp.maximum(jnp.abs(den), gt[s]['inv_floor'][:, h:h + 1])
        cen = num[s, h] - jnp.mean(num[s, h], axis=1, keepdims=True)
        var = jnp.mean(cen * cen, axis=1, keepdims=True)
        hn = cen * (r * lax.rsqrt(r * r * var + LN_EPS)) * hng_ref[:, c0:c1]
        out[s, h] = ((hn + skip_ref[:, c0:c1] * block('xc', s, h))
                     * block('sz', s, h)).astype(BF16)

        decay = gt[s]['decay'][:, h:h + 1]
        kw = block('ks', s, h) * gt[s]['w'][:, h:h + 1]
        kwb = kw.astype(BF16)
        for r0 in range(0, dh, MXU_DIM):
            r1 = r0 + MXU_DIM
            upd = lax.dot_general(kwb[:, r0:r1], block('vb', s, h), (((0,), (0,)), ((), ())),
                                  preferred_element_type=F32)
            c_ref[s, h, r0:r1, :] = decay * cin_ref[s, h, r0:r1, :] + upd
        n_new[s, h] = decay * n_prev + jnp.sum(kw, axis=0, keepdims=True)
    for s in range(ns):
        n_ref[s] = jnp.concatenate([n_new[s, h] for h in range(HEADS)], axis=0)
    yield

    y_rows = []
    for s in range(ns):
        acc = [None] * n_out
        for h in range(HEADS):
            for n in range(n_out):
                part = _dot(out[s, h], wout_ref[n, h * dh:(h + 1) * dh, :])
                acc[n] = part if acc[n] is None else acc[n] + part
        y_rows.append(jnp.concatenate(acc, axis=1))
    proj = y_rows[0] if ns == 1 else jnp.concatenate(y_rows, axis=0)
    y_ref[...] = _rmsnorm(xres_ref[...] + proj, fg_ref[...])


def _mlstm_pipelined_kernel(x_ref, xprev_ref, ng_ref, fg_ref, win_ref, cw_ref, cb_ref,
                            bd_ref, wg_ref, bg_ref, hng_ref, skip_ref,
                            wout_ref, y_ref, c_ref, n_ref, m_ref, conv_ref, xpad_ref,
                            *handoff, seg, n_tiles, total):
    t = pl.program_id(0)
    e = D_INNER
    pad = SUBLANES
    n_hand = len(_HANDOFF)
    sets = (dict(zip(_HANDOFF_NAMES, handoff[:n_hand])),
            dict(zip(_HANDOFF_NAMES, handoff[n_hand:])))

    @pl.when(t % n_tiles == 0)
    def _():
        xpad_ref[:, 0:pad, :] = jnp.zeros((1, pad, e), F32)

    @pl.when((t - 1) % n_tiles == 0)
    def _():
        c_ref[...] = jnp.zeros(c_ref.shape, F32)
        n_ref[...] = jnp.zeros(n_ref.shape, F32)
        m_ref[...] = jnp.zeros(m_ref.shape, F32)

    def step(dst, src):
        gens = []
        if src is not None:
            gens.append(_backend(xprev_ref, src, c_ref, n_ref, m_ref, hng_ref, skip_ref,
                                 wout_ref, fg_ref, y_ref, c_ref, n_ref, m_ref, ns=1, seg=seg))
        if dst is not None:
            gens.append(_frontend(x_ref, ng_ref, win_ref, cw_ref, cb_ref, bd_ref.at[0],
                                  bd_ref.at[1], bd_ref.at[2], wg_ref, bg_ref, xpad_ref, dst,
                                  ns=1, seg=seg))
        _interleave(*gens)
        if dst is not None:
            conv_ref[...] = xpad_ref[:, pad + seg - (CONV_W - 1):pad + seg, :]
            xpad_ref[:, 0:pad, :] = xpad_ref[:, seg:seg + pad, :]

    @pl.when(t == 0)
    def _():
        step(sets[0], None)

    @pl.when(t == total)
    def _():
        step(None, sets[(total - 1) % 2])

    inner = (t > 0) & (t < total)

    @pl.when(inner & (t % 2 == 0))
    def _():
        step(sets[0], sets[1])

    @pl.when(inner & (t % 2 == 1))
    def _():
        step(sets[1], sets[0])


def _mlstm_front_kernel(x_ref, ng_ref, win_ref, cw_ref, cb_ref, bd_ref, wg_ref, bg_ref,
                        conv0_ref, *rest, ns, seg):
    n_hand = len(_HANDOFF)
    dst = dict(zip(_HANDOFF_NAMES, rest[:n_hand]))
    conv_ref, xpad_ref = rest[n_hand:]
    pad = SUBLANES
    for j in range(CONV_W - 1):
        xpad_ref[:, pad - (CONV_W - 1) + j, :] = conv0_ref[j]
    _interleave(_frontend(x_ref, ng_ref, win_ref, cw_ref, cb_ref, bd_ref.at[0], bd_ref.at[1],
                          bd_ref.at[2], wg_ref, bg_ref, xpad_ref, dst, ns=ns, seg=seg))
    for j in range(CONV_W - 1):
        conv_ref[j] = xpad_ref[:, pad + seg - (CONV_W - 1) + j, :]


def _sgu_recurrence_kernel(*refs, tm, seg):
    n_hand = len(_HANDOFF)
    sgu_refs, rest = refs[:8], refs[8:]
    xres_ref = rest[0]
    src = dict(zip(_HANDOFF_NAMES, rest[1:1 + n_hand]))
    (c0_ref, n0_ref, m0_ref, hng_ref, skip_ref, wout_ref, fg_ref,
     y_ref, ys_ref, c_ref, n_ref, m_ref) = rest[1 + n_hand:]
    _interleave(
        _backend(xres_ref, src, c0_ref, n0_ref, m0_ref, hng_ref, skip_ref, wout_ref,
                 fg_ref, ys_ref, c_ref, n_ref, m_ref, ns=1, seg=seg, memory_read_first=True),
        _sgu_stages(*sgu_refs, y_ref, (), tm=tm))


def _state_specs(ns, index):
    dh = HEAD_DIM
    return [pl.BlockSpec((ns, HEADS, dh, dh), lambda *i: (index(*i), 0, 0, 0)),
            pl.BlockSpec((ns, HEADS, dh), lambda *i: (index(*i), 0, 0)),
            pl.BlockSpec((ns, 1, HEADS), lambda *i: (index(*i), 0, 0))]


def _state_shapes(n_streams):
    dh = HEAD_DIM
    return [jax.ShapeDtypeStruct((n_streams, HEADS, dh, dh), F32),
            jax.ShapeDtypeStruct((n_streams, HEADS, dh), F32),
            jax.ShapeDtypeStruct((n_streams, 1, HEADS), F32)]


def _front_weight_args(w):
    return [w['norm_g'], w['w_in'], w['conv_w'], w['conv_b'], w['bd'], w['wg'], w['bg']]


def _front_weight_specs(w):
    return [_const_spec(a.shape) for a in _front_weight_args(w)]


def _mlstm_prompt(x, w, *, n_streams, seg):
    e = D_INNER
    n_tiles = x.shape[0] // (n_streams * seg)
    total = n_streams * n_tiles

    def front_tile(t):
        return jnp.minimum(t, total - 1)

    def back_tile(t):
        return jnp.maximum(t - 1, 0)

    cur_spec = pl.BlockSpec((seg, D_MODEL), lambda t: (front_tile(t), 0))
    prev_spec = pl.BlockSpec((seg, D_MODEL), lambda t: (back_tile(t), 0))
    conv_spec = pl.BlockSpec((1, CONV_W - 1, e), lambda t: (front_tile(t) // n_tiles, 0, 0))
    back_specs = [_const_spec((1, e)), _const_spec((1, e)), _const_spec(w['w_out'].shape)]
    fw = _front_weight_specs(w)
    in_specs = [cur_spec, prev_spec, fw[0], _const_spec((1, D_MODEL))] + fw[1:] + back_specs
    fa = _front_weight_args(w)
    args = [x, x, fa[0], w['final_g']] + fa[1:] + [w['hn_g'], w['skip'], w['w_out']]
    handoff = [pltpu.VMEM((seg, width), dtype)
               for _ in range(2) for _, width, dtype in _HANDOFF]
    return pl.pallas_call(
        functools.partial(_mlstm_pipelined_kernel, seg=seg, n_tiles=n_tiles, total=total),
        grid=(total + 1,),
        in_specs=in_specs,
        out_specs=([prev_spec] + _state_specs(1, lambda t: back_tile(t) // n_tiles)
                   + [conv_spec]),
        out_shape=([jax.ShapeDtypeStruct(x.shape, F32)] + _state_shapes(n_streams)
                   + [jax.ShapeDtypeStruct((n_streams, CONV_W - 1, e), F32)]),
        scratch_shapes=[pltpu.VMEM((1, SUBLANES + seg, e), F32)] + handoff,
        compiler_params=pltpu.CompilerParams(
            dimension_semantics=("arbitrary",),
            vmem_limit_bytes=VMEM_LIMIT_BYTES),
        name="mlstm_prompt",
    )(*args)


def _mlstm_sample_front(x, w, conv0, *, seg, ns_front):
    e = D_INNER
    n_streams = conv0.shape[1]
    rows_f = ns_front * seg
    hand_shapes = [jax.ShapeDtypeStruct((x.shape[0], width), dtype)
                   for _, width, dtype in _HANDOFF]
    conv_spec = pl.BlockSpec((CONV_W - 1, ns_front, e), lambda i: (0, i, 0))
    res = pl.pallas_call(
        functools.partial(_mlstm_front_kernel, ns=ns_front, seg=seg),
        grid=(n_streams // ns_front,),
        in_specs=([pl.BlockSpec((rows_f, D_MODEL), lambda i: (i, 0))]
                  + _front_weight_specs(w) + [conv_spec]),
        out_specs=([pl.BlockSpec((rows_f, width), lambda i: (i, 0)) for _, width, _ in _HANDOFF]
                   + [conv_spec]),
        out_shape=hand_shapes + [jax.ShapeDtypeStruct((CONV_W - 1, n_streams, e), F32)],
        scratch_shapes=[pltpu.VMEM((ns_front, SUBLANES + seg, e), F32)],
        compiler_params=pltpu.CompilerParams(
            dimension_semantics=("arbitrary",), vmem_limit_bytes=VMEM_LIMIT_BYTES),
        name="mlstm_sample_front",
    )(x, *_front_weight_args(w), conv0)
    return res[:-1], res[-1]


def _sgu_prompt_with_sample_recurrence(xp, sgu_args, xs, hand, state, w, *, seg):
    e = D_INNER
    c0, n0, m0 = state
    n_streams = c0.shape[0]
    assert xp.shape[0] % n_streams == 0 and (xp.shape[0] // n_streams) % SGU_CHUNK == 0
    tm = xp.shape[0] // n_streams
    norm_g, w_in, ln_g, ln_b, w_s, b_s, w_out = sgu_args

    sgu_specs = [pl.BlockSpec((tm, D_MODEL), lambda i: (i, 0)),
                 _const_spec((1, D_MODEL)), _const_spec(w_in.shape),
                 _const_spec((1, e)), _const_spec((1, e)),
                 _const_spec((SGU_GROUPS, SGU_CHUNK, SGU_CHUNK)),
                 _const_spec((SGU_CHUNK, SGU_GROUPS)), _const_spec(w_out.shape)]
    hand_specs = [pl.BlockSpec((seg, width), lambda i: (i, 0)) for _, width, _ in _HANDOFF]
    state_specs = _state_specs(1, lambda i: i)
    ys_spec = pl.BlockSpec((seg, D_MODEL), lambda i: (i, 0))
    xp1, ys, c_out, n_out, m_out = pl.pallas_call(
        functools.partial(_sgu_recurrence_kernel, tm=tm, seg=seg),
        grid=(n_streams,),
        in_specs=(sgu_specs + [ys_spec] + hand_specs + state_specs
                  + [_const_spec((1, e)), _const_spec((1, e)), _const_spec(w['w_out'].shape),
                     _const_spec((1, D_MODEL))]),
        out_specs=[pl.BlockSpec((tm, D_MODEL), lambda i: (i, 0)), ys_spec] + state_specs,
        out_shape=([jax.ShapeDtypeStruct(xp.shape, F32), jax.ShapeDtypeStruct(xs.shape, F32)]
                   + _state_shapes(n_streams)),
        compiler_params=pltpu.CompilerParams(
            dimension_semantics=("arbitrary",), vmem_limit_bytes=VMEM_LIMIT_BYTES),
        name="sgu_prompt_sample_recurrence",
    )(xp, norm_g, w_in, ln_g, ln_b, w_s, b_s, w_out, xs, *hand, c0, n0, m0,
      w['hn_g'], w['skip'], w['w_out'], w['final_g'])
    return xp1, ys, c_out, n_out, m_out


def _qkv_weights_kernel(rows_ref, wg_ref, bd_ref, fold_ref):
    nt = rows_ref.shape[1]
    r_blk = lax.broadcasted_iota(jnp.int32, (MXU_DIM, MXU_DIM), 0) // QKV_BLOCK
    c_idx = lax.broadcasted_iota(jnp.int32, (MXU_DIM, MXU_DIM), 1)
    same_block = r_blk == c_idx // QKV_BLOCK
    for j in range(nt):
        r0, r1 = j * MXU_DIM, (j + 1) * MXU_DIM
        tiles = []
        for a in range(3):
            rows = rows_ref[a, j]
            tiled = jnp.zeros((MXU_DIM, MXU_DIM), F32)
            for o in range(QKV_BLOCK):
                tiled = jnp.where(c_idx % QKV_BLOCK == o, rows[:, o:o + 1], tiled)
            tile = jnp.where(same_block, tiled, 0.0).astype(BF16)
            bd_ref[a, j] = tile
            tiles.append(tile)
        fold_ref[0, r0:r1, :] = (_dot(tiles[0], wg_ref[0, r0:r1, :])
                                 + _dot(tiles[1], wg_ref[1, r0:r1, :])).astype(BF16)
        fold_ref[1, r0:r1, :] = _dot(tiles[2], wg_ref[2, r0:r1, :]).astype(BF16)


def _qkv_weights(wq, wk, wv, wg):
    nt = D_INNER // MXU_DIM
    rows = jnp.stack([wq, wk, wv]).reshape(3, nt, MXU_DIM, QKV_BLOCK)
    return pl.pallas_call(
        _qkv_weights_kernel,
        out_shape=[jax.ShapeDtypeStruct((3, nt, MXU_DIM, MXU_DIM), BF16),
                   jax.ShapeDtypeStruct((2, D_INNER, wg.shape[-1]), BF16)],
        name="qkv_weights",
    )(rows, wg)


def _head_lanes(w):
    lead = w.shape[:-1]
    z = jnp.zeros(lead + (LANES - HEADS,), w.dtype)
    return jnp.concatenate([w[..., :HEADS], z, w[..., HEADS:], z], axis=-1)


def kernel(x_prompt, x_sample, state_mlstm_C, state_mlstm_n, state_mlstm_m, state_mlstm_conv, norm_g, final_norm_g, a_w_in, a_ln_g, a_ln_b, a_w_s, a_b_s, a_w_out, b_w_in, b_conv_w, b_conv_b, b_wq, b_wk, b_wv, b_w_gates, b_b_gates, b_hnorm_g, b_skip, b_w_out):
    bsz, seq, d = x_prompt.shape
    dec_b, dec_t, _ = x_sample.shape
    e = D_INNER
    assert norm_g.shape[0] == 2 and a_w_in.shape[0] == 1 and b_w_in.shape[0] == 1
    assert d == D_MODEL and seq % MXU_DIM == 0 and SGU_CHUNK % dec_t == 0

    xp = x_prompt.reshape(bsz * seq, d)
    xs = x_sample.reshape(dec_b * dec_t, d)

    a_in = _col_slabs(a_w_in[0])
    a_out = _col_slabs(a_w_out[0])
    ng0 = norm_g[0].reshape(1, d)
    lng = a_ln_g[0].reshape(1, e)
    lnb = a_ln_b[0].reshape(1, e)
    per_tile = SGU_CHUNK // dec_t
    w_head = a_w_s[0][:, :dec_t, :dec_t]
    ws_s = jnp.einsum('ab,gts->gatbs', np.eye(per_tile, dtype=np.float32), w_head)
    ws_s = ws_s.reshape(SGU_GROUPS, SGU_CHUNK, SGU_CHUNK)
    bs_s = jnp.tile(a_b_s[0][:, :dec_t], (1, per_tile)).T
    xs1, v_s = _sgu_layer(xs, ng0, a_in, lng, lnb, ws_s, bs_s, a_out,
                          tm=2 * SGU_CHUNK, want_v=True)

    wg = _head_lanes(b_w_gates[0]).astype(BF16).reshape(3, e, 2 * LANES)
    bd, wg_folded = _qkv_weights(b_wq[0], b_wk[0], b_wv[0], wg)
    w = dict(norm_g=norm_g[1].reshape(1, d), final_g=final_norm_g.reshape(1, d),
             w_in=_col_slabs(b_w_in[0]), conv_w=b_conv_w[0], conv_b=b_conv_b[0].reshape(1, e),
             bd=bd, wg=wg_folded,
             bg=_head_lanes(b_b_gates[0]).reshape(1, 2 * LANES),
             hn_g=b_hnorm_g[0].reshape(1, e), skip=b_skip[0].reshape(1, e),
             w_out=_col_slabs(b_w_out[0]))
    hand, conv_s = _mlstm_sample_front(xs1, w, state_mlstm_conv[0].transpose(1, 0, 2),
                                       seg=dec_t, ns_front=dec_b // 2)
    conv_s = conv_s.transpose(1, 0, 2)
    m0 = state_mlstm_m[0].reshape(dec_b, 1, HEADS)
    xp1, ys, c_s, n_s, m_s = _sgu_prompt_with_sample_recurrence(
        xp, (ng0, a_in, lng, lnb, a_w_s[0], a_b_s[0].T, a_out), xs1, hand,
        (state_mlstm_C[0], state_mlstm_n[0], m0), w, seg=dec_t)
    yp, c_p, n_p, m_p, conv_p = _mlstm_prompt(xp1, w, n_streams=bsz, seg=MXU_DIM)

    return (yp.reshape(bsz, seq, d),
            ys.reshape(dec_b, dec_t, d),
            v_s.reshape(1, dec_b, dec_t, e),
            c_p[None], n_p[None], m_p.reshape(1, bsz, HEADS), conv_p[None],
            c_s[None], n_s[None], m_s.reshape(1, dec_b, HEADS), conv_s[None])
```
